```python
import jax, jax.numpy as jnp
from jax import lax
import numpy as np

D_MODEL = 1024
BATCH = 8
SEQ = 8192
DEPTH = 4

A_W = D_MODEL // 2
CONV_A = 31
B_W = D_MODEL // 2
CONV_B = 3
C_HEADS = 8
C_HEAD_DIM = 64
C_W = C_HEADS * C_HEAD_DIM
Q_BLOCK = 128
M_HEADS = 4
M_HEAD_DIM = 64
M_W = M_HEADS * M_HEAD_DIM
MEM_LEN = 256
N_BRANCH = 4
ALPHA = (2.0 * DEPTH) ** 0.25
BETA = (8.0 * DEPTH) ** -0.25
LN_EPS = 1e-5

SPLITS = (
    2 * A_W, A_W,
    B_W, B_W, B_W, B_W,
    C_W, C_W, C_W, C_HEADS, C_W,
    M_W, M_W,
    N_BRANCH * D_MODEL,
)
IN_COLS = sum(SPLITS)
SPLIT_CUTS = tuple(int(c) for c in np.cumsum(SPLITS)[:-1])

kernel_name = 'hybrid_gated_conformer_shortconv_fox_memxattn'


def _layer_norm(x, g, b):
    xf = x.astype(jnp.float32)
    mu = jnp.mean(xf, axis=-1, keepdims=True)
    var = jnp.mean(jnp.square(xf - mu), axis=-1, keepdims=True)
    y = ((xf - mu) * lax.rsqrt(var + LN_EPS)).astype(x.dtype)
    return y * g + b


def _causal_dwconv(x, w):
    k_width, ch = w.shape
    return lax.conv_general_dilated(
        x, w[:, None, :].astype(x.dtype), window_strides=(1,),
        padding=[(k_width - 1, 0)],
        dimension_numbers=('NWC', 'WIO', 'NWC'),
        feature_group_count=ch)


def _forgetting_attention(q, k, v, logf):
    bsz, seq, heads, dh = q.shape
    scale = dh ** -0.5
    cum = jnp.cumsum(logf, axis=1).transpose(0, 2, 1)
    outs = []
    for i in range(seq // Q_BLOCK):
        q0, end = i * Q_BLOCK, (i + 1) * Q_BLOCK
        s = jnp.einsum('bqhd,bkhd->bhqk', q[:, q0:end], k[:, :end]).astype(jnp.float32) * scale
        decay = cum[:, :, q0:end, None] - cum[:, :, None, :end]
        q_pos = q0 + jnp.arange(Q_BLOCK)
        k_pos = jnp.arange(end)
        causal = q_pos[:, None] >= k_pos[None, :]
        s = jnp.where(causal, s + decay, -jnp.inf)
        p = jax.nn.softmax(s, axis=-1).astype(v.dtype)
        outs.append(jnp.einsum('bhqk,bkhd->bqhd', p, v[:, :end]))
    return jnp.concatenate(outs, axis=1)


def _cross_attention(q, k, v):
    dh = q.shape[-1]
    s = jnp.einsum('bshd,bmhd->bhsm', q, k).astype(jnp.float32) * (dh ** -0.5)
    p = jax.nn.softmax(s, axis=-1).astype(v.dtype)
    return jnp.einsum('bhsm,bmhd->bshd', p, v)


def setup_inputs(seed: int = 0) -> dict:
    key = jax.random.key(seed)
    ks = jax.random.split(key, 20)
    f32 = jnp.float32
    nrm = lambda k, shape, scale: jax.random.normal(k, shape, f32) * scale
    return {
        'x': nrm(ks[0], (BATCH, SEQ, D_MODEL), 1.0),
        'mem': nrm(ks[1], (BATCH, MEM_LEN, D_MODEL), 1.0),
        'w_in': nrm(ks[2], (DEPTH, D_MODEL, IN_COLS), D_MODEL ** -0.5),
        'b_forget': jax.random.uniform(ks[3], (DEPTH, C_HEADS), f32, 1.0, 5.0),
        'conv_a_w': nrm(ks[4], (DEPTH, CONV_A, A_W), CONV_A ** -0.5),
        'conv_a_b': nrm(ks[5], (DEPTH, A_W), 0.02),
        'ln_a_g': 1.0 + nrm(ks[6], (DEPTH, A_W), 0.05),
        'ln_a_b': nrm(ks[7], (DEPTH, A_W), 0.02),
        'conv_b_w': nrm(ks[8], (DEPTH, CONV_B, B_W), CONV_B ** -0.5),
        'w_kv_mem': nrm(ks[9], (DEPTH, D_MODEL, 2 * M_W), D_MODEL ** -0.5),
        'mem_ln_g': 1.0 + nrm(ks[10], (D_MODEL,), 0.05),
        'mem_ln_b': nrm(ks[11], (D_MODEL,), 0.02),
        'p_a': nrm(ks[12], (DEPTH, A_W, D_MODEL), BETA * A_W ** -0.5),
        'p_b': nrm(ks[13], (DEPTH, B_W, D_MODEL), BETA * B_W ** -0.5),
        'p_c': nrm(ks[14], (DEPTH, C_W, D_MODEL), BETA * C_W ** -0.5),
        'p_m': nrm(ks[15], (DEPTH, M_W, D_MODEL), BETA * M_W ** -0.5),
        'w_out': nrm(ks[16], (DEPTH, D_MODEL, D_MODEL), BETA * D_MODEL ** -0.5),
        'ln_g': 1.0 + nrm(ks[17], (DEPTH, D_MODEL), 0.05),
        'ln_b': nrm(ks[18], (DEPTH, D_MODEL), 0.02),
    }


def reference(x, mem, w_in, b_forget, conv_a_w, conv_a_b, ln_a_g, ln_a_b, conv_b_w,
              w_kv_mem, mem_ln_g, mem_ln_b, p_a, p_b, p_c, p_m, w_out, ln_g, ln_b):
    bsz, seq, _ = x.shape
    mem_n = _layer_norm(mem, mem_ln_g, mem_ln_b)
    for l in range(DEPTH):
        proj = x @ w_in[l]
        (a_glu, a_gate, b_h, b_b, b_c, b_gate, c_q, c_k, c_v, c_f, c_gate,
         m_q, m_gate, g) = jnp.split(proj, SPLIT_CUTS, axis=-1)

        a_u, a_v = jnp.split(a_glu, 2, axis=-1)
        a = _causal_dwconv(a_u * jax.nn.sigmoid(a_v), conv_a_w[l]) + conv_a_b[l]
        a = jax.nn.silu(_layer_norm(a, ln_a_g[l], ln_a_b[l]))
        y_a = (a * jax.nn.silu(a_gate)) @ p_a[l]

        hb = b_b * _causal_dwconv(b_c * b_h, conv_b_w[l])
        y_b = (hb * jax.nn.silu(b_gate)) @ p_b[l]

        q = c_q.reshape(bsz, seq, C_HEADS, C_HEAD_DIM)
        k = c_k.reshape(bsz, seq, C_HEADS, C_HEAD_DIM)
        v = c_v.reshape(bsz, seq, C_HEADS, C_HEAD_DIM)
        logf = jax.nn.log_sigmoid((c_f + b_forget[l]).astype(jnp.float32))
        o_c = _forgetting_attention(q, k, v, logf).reshape(bsz, seq, C_W)
        y_c = (o_c * jax.nn.silu(c_gate)) @ p_c[l]

        mk, mv = jnp.split(mem_n @ w_kv_mem[l], 2, axis=-1)
        o_m = _cross_attention(
            m_q.reshape(bsz, seq, M_HEADS, M_HEAD_DIM),
            mk.reshape(bsz, MEM_LEN, M_HEADS, M_HEAD_DIM),
            mv.reshape(bsz, MEM_LEN, M_HEADS, M_HEAD_DIM)).reshape(bsz, seq, M_W)
        y_m = (o_m * jax.nn.silu(m_gate)) @ p_m[l]

        gates = jax.nn.sigmoid(g.reshape(bsz, seq, N_BRANCH, D_MODEL))
        merged = (gates[:, :, 0] * y_a + gates[:, :, 1] * y_b
                  + gates[:, :, 2] * y_c + gates[:, :, 3] * y_m)
        out = merged @ w_out[l]

        x = _layer_norm(ALPHA * x + out, ln_g[l], ln_b[l])
    return x
```

```python
import functools

import jax
import jax.numpy as jnp
from jax import lax
from jax.experimental import pallas as pl
from jax.experimental.pallas import tpu as pltpu

F32 = jnp.float32
BF16 = jnp.bfloat16

LN_EPS = 1e-5
HEAD_DIM = 64
LANES = 128
SUBLANES = 8
BF16_ROWS = 16
VMEM_LIMIT_BYTES = 56 * 1024 * 1024
NEG_BIG = -1e30

TOKEN_TILE = 512
ATTN_TILE = 256
CONV_ROW_CHUNK = 32
A_HIST = 32
B_HIST = 8


def _dot(a, b):
    return jnp.dot(a, b, preferred_element_type=F32)


def _dot_nt(a, b):
    return lax.dot_general(a, b, (((1,), (1,)), ((), ())), preferred_element_type=F32)


def _sigmoid(v):
    return 1.0 / (1.0 + jnp.exp(-v))


def _silu(v):
    return v * _sigmoid(v)


def _log_sigmoid(v):
    return jnp.minimum(v, 0.0) - jnp.log1p(jnp.exp(-jnp.abs(v)))


def _layer_norm(v, g, b):
    mu = jnp.mean(v, axis=-1, keepdims=True)
    c = v - mu
    var = jnp.mean(c * c, axis=-1, keepdims=True)
    return c * lax.rsqrt(var + LN_EPS) * g + b


def _split3(v):
    hi = v.astype(BF16)
    r1 = v - hi.astype(F32)
    mid = r1.astype(BF16)
    lo = (r1 - mid.astype(F32)).astype(BF16)
    return hi, mid, lo


def _const_spec(shape, layer):
    nd = len(shape)
    return pl.BlockSpec((None,) + tuple(shape), lambda *_, l=layer, nd=nd: (l,) + (0,) * nd,
                        pipeline_mode=pl.Buffered(1))


def _mem_kernel(mem_ref, g_ref, b_ref, wkT_ref, wv_ref, mkT_ref, mv_ref, *, n_heads):
    mem_n = _layer_norm(mem_ref[...], g_ref[...], b_ref[...]).astype(BF16)
    mkT = _dot_nt(wkT_ref[...], mem_n)
    mv = _dot(mem_n, wv_ref[...])
    row_head = lax.broadcasted_iota(jnp.int32, mkT.shape, 0) // HEAD_DIM
    col_head = lax.broadcasted_iota(jnp.int32, mv.shape, 1) // HEAD_DIM
    for h in range(n_heads):
        mkT_ref[h] = jnp.where(row_head == h, mkT, 0.0).astype(BF16)
        mv_ref[h] = jnp.where(col_head == h, mv, 0.0).astype(BF16)


def _mem_call(mem, g, b, wkT, wv):
    bsz, mlen, d = mem.shape
    depth, mw, _ = wkT.shape
    n_heads = mw // HEAD_DIM
    out = jax.ShapeDtypeStruct((bsz, depth, n_heads, mw, mlen), BF16), \
        jax.ShapeDtypeStruct((bsz, depth, n_heads, mlen, mw), BF16)
    return pl.pallas_call(
        functools.partial(_mem_kernel, n_heads=n_heads),
        grid=(bsz, depth),
        in_specs=[
            pl.BlockSpec((None, mlen, d), lambda i, l: (i, 0, 0)),
            pl.BlockSpec((1, d), lambda i, l: (0, 0)),
            pl.BlockSpec((1, d), lambda i, l: (0, 0)),
            pl.BlockSpec((None, mw, d), lambda i, l: (l, 0, 0)),
            pl.BlockSpec((None, d, mw), lambda i, l: (l, 0, 0)),
        ],
        out_specs=[
            pl.BlockSpec((None, None, n_heads, mw, mlen), lambda i, l: (i, l, 0, 0, 0)),
            pl.BlockSpec((None, None, n_heads, mlen, mw), lambda i, l: (i, l, 0, 0, 0)),
        ],
        out_shape=out,
        compiler_params=pltpu.CompilerParams(dimension_semantics=("arbitrary", "arbitrary"),
                                             vmem_limit_bytes=VMEM_LIMIT_BYTES),
        name="mem_kv",
    )(mem, g, b, wkT, wv)


def _proj_kernel(x_ref, wa_ref, wb_ref, wkc_ref, wqvT_ref, wm_ref, bfn_ref, bfT_ref,
                 caw_ref, cab_ref, lag_ref, lab_ref, cbw_ref, mkT_ref, mv_ref, tril_ref, triu_ref,
                 ha_ref, hb_ref, k_ref, sgc_ref, qT_ref, vT_ref, cum_ref, cumT_ref, hm_ref,
                 abuf, ashift, aconv, bbuf, carry_n, carry_t,
                 *, tm, aw, bw, cw, mw, n_cheads, taps_a, taps_b):
    @pl.when(pl.program_id(1) == 0)
    def _():
        abuf[0:A_HIST, :] = jnp.zeros((A_HIST, aw), F32)
        bbuf[0:B_HIST, :] = jnp.zeros((B_HIST, bw), F32)
        carry_n[...] = jnp.zeros_like(carry_n)
        carry_t[...] = jnp.zeros_like(carry_t)

    xb = x_ref[...].astype(BF16)

    xa = _dot(xb, wa_ref[...])
    abuf[A_HIST:A_HIST + tm, :] = xa[:, :aw] * _sigmoid(xa[:, aw:2 * aw])
    n_shift_rows = tm + A_HIST - SUBLANES
    for sh in range(1, SUBLANES):
        ashift[sh - 1] = abuf[sh:sh + n_shift_rows, :]
    first = A_HIST - (taps_a - 1)

    def conv_chunk(c, carry):
        r0 = pl.multiple_of(c * CONV_ROW_CHUNK, CONV_ROW_CHUNK)
        acc = jnp.broadcast_to(cab_ref[...], (CONV_ROW_CHUNK, aw))
        for t in range(taps_a):
            blk, sh = divmod(first + t, SUBLANES)
            start = r0 + blk * SUBLANES
            if sh == 0:
                rows = abuf[pl.ds(start, CONV_ROW_CHUNK), :]
            else:
                rows = ashift[sh - 1, pl.ds(start, CONV_ROW_CHUNK), :]
            acc = acc + caw_ref[t:t + 1, :] * rows
        aconv[pl.ds(r0, CONV_ROW_CHUNK), :] = acc
        return carry

    lax.fori_loop(0, tm // CONV_ROW_CHUNK, conv_chunk, 0)
    abuf[0:A_HIST, :] = abuf[tm:tm + A_HIST, :]
    a = _silu(_layer_norm(aconv[...], lag_ref[...], lab_ref[...]))
    ha_ref[...] = (a * _silu(xa[:, 2 * aw:])).astype(BF16)

    xbm = _dot(xb, wb_ref[...])
    bbuf[B_HIST:B_HIST + tm, :] = xbm[:, 2 * bw:3 * bw] * xbm[:, :bw]
    firstb = B_HIST - (taps_b - 1)
    convb = cbw_ref[0:1, :] * bbuf[firstb:firstb + tm, :]
    for t in range(1, taps_b):
        convb = convb + cbw_ref[t:t + 1, :] * bbuf[firstb + t:firstb + t + tm, :]
    bbuf[0:B_HIST, :] = bbuf[tm:tm + B_HIST, :]
    hb_ref[...] = (xbm[:, bw:2 * bw] * convb * _silu(xbm[:, 3 * bw:])).astype(BF16)

    xk = _dot(xb, wkc_ref[...])
    k_ref[...] = xk[:, :cw].astype(BF16)
    sgc_ref[...] = _silu(xk[:, cw:2 * cw]).astype(BF16)
    lf = _log_sigmoid(xk[:, 2 * cw:] + bfn_ref[...])
    hi, mid, lo = _split3(lf)
    tril = tril_ref[...]
    cum = _dot(tril, hi) + _dot(tril, mid) + _dot(tril, lo) + carry_n[...]
    carry_n[...] = cum[tm - 1:tm, :]
    cum_ref[...] = cum[:, :n_cheads]

    qv = _dot_nt(wqvT_ref[...], xb)
    qT_ref[...] = (qv[:cw] * (HEAD_DIM ** -0.5)).astype(BF16)
    vT_ref[...] = qv[cw:2 * cw].astype(BF16)
    lft = _log_sigmoid(qv[2 * cw:] + bfT_ref[...][:, 0:1])
    hi, mid, lo = _split3(lft)
    triu = triu_ref[...]
    cumt = _dot(hi, triu) + _dot(mid, triu) + _dot(lo, triu) + carry_t[...][:, 0:1]
    carry_t[...] = jnp.broadcast_to(cumt[:, tm - 1:tm], carry_t.shape)
    cumT_ref[...] = cumt[:n_cheads]

    xm = _dot(xb, wm_ref[...])
    mq = (xm[:, :mw] * (HEAD_DIM ** -0.5)).astype(BF16)
    om = jnp.zeros((tm, mw), F32)
    for h in range(mw // HEAD_DIM):
        s = _dot(mq, mkT_ref[h])
        e = jnp.exp(s - jnp.max(s, axis=-1, keepdims=True))
        p = e / jnp.sum(e, axis=-1, keepdims=True)
        om = om + _dot(p.astype(BF16), mv_ref[h])
    hm_ref[...] = (om * _silu(xm[:, mw:])).astype(BF16)


def _proj_call(x, layer, w, mkT, mv, tril, triu, dims):
    bsz, seq, d = x.shape
    tm = min(TOKEN_TILE, seq)
    aw, bw, cw, mw, nch = dims["aw"], dims["bw"], dims["cw"], dims["mw"], dims["n_cheads"]
    taps_a, taps_b = w["caw"].shape[1], w["cbw"].shape[1]
    n_heads_m = mw // HEAD_DIM
    mlen = mv.shape[3]
    row = lambda width: pl.BlockSpec((None, tm, width), lambda i, s: (i, s, 0))
    col = lambda height: pl.BlockSpec((None, height, tm), lambda i, s: (i, 0, s))
    in_specs = [
        row(d),
        _const_spec(w["wa"].shape[1:], layer), _const_spec(w["wb"].shape[1:], layer),
        _const_spec(w["wkc"].shape[1:], layer), _const_spec(w["wqvT"].shape[1:], layer),
        _const_spec(w["wm"].shape[1:], layer),
        _const_spec(w["bfn"].shape[1:], layer), _const_spec(w["bfT"].shape[1:], layer),
        _const_spec(w["caw"].shape[1:], layer), _const_spec(w["cab"].shape[1:], layer),
        _const_spec(w["lag"].shape[1:], layer), _const_spec(w["lab"].shape[1:], layer),
        _const_spec(w["cbw"].shape[1:], layer),
        pl.BlockSpec((None, None, n_heads_m, mw, mlen), lambda i, s, l=layer: (i, l, 0, 0, 0)),
        pl.BlockSpec((None, None, n_heads_m, mlen, mw), lambda i, s, l=layer: (i, l, 0, 0, 0)),
        pl.BlockSpec((tm, tm), lambda i, s: (0, 0), pipeline_mode=pl.Buffered(1)),
        pl.BlockSpec((tm, tm), lambda i, s: (0, 0), pipeline_mode=pl.Buffered(1)),
    ]
    out_specs = [row(aw), row(bw), row(cw), row(cw), col(cw), col(cw), row(nch), col(nch), row(mw)]
    out_shape = [
        jax.ShapeDtypeStruct((bsz, seq, aw), BF16), jax.ShapeDtypeStruct((bsz, seq, bw), BF16),
        jax.ShapeDtypeStruct((bsz, seq, cw), BF16), jax.ShapeDtypeStruct((bsz, seq, cw), BF16),
        jax.ShapeDtypeStruct((bsz, cw, seq), BF16), jax.ShapeDtypeStruct((bsz, cw, seq), BF16),
        jax.ShapeDtypeStruct((bsz, seq, nch), F32), jax.ShapeDtypeStruct((bsz, nch, seq), F32),
        jax.ShapeDtypeStruct((bsz, seq, mw), BF16),
    ]
    scratch = [
        pltpu.VMEM((A_HIST + tm, aw), F32),
        pltpu.VMEM((SUBLANES - 1, tm + A_HIST - SUBLANES, aw), F32),
        pltpu.VMEM((tm, aw), F32),
        pltpu.VMEM((B_HIST + tm, bw), F32),
        pltpu.VMEM((1, LANES), F32),
        pltpu.VMEM((BF16_ROWS, LANES), F32),
    ]
    kern = functools.partial(_proj_kernel, tm=tm, aw=aw, bw=bw, cw=cw, mw=mw, n_cheads=nch,
                             taps_a=taps_a, taps_b=taps_b)
    return pl.pallas_call(
        kern, grid=(bsz, seq // tm), in_specs=in_specs, out_specs=out_specs, out_shape=out_shape,
        scratch_shapes=scratch,
        compiler_params=pltpu.CompilerParams(dimension_semantics=("arbitrary", "arbitrary"),
                                             vmem_limit_bytes=VMEM_LIMIT_BYTES),
        name="proj_branches",
    )(x, w["wa"], w["wb"], w["wkc"], w["wqvT"], w["wm"], w["bfn"], w["bfT"],
      w["caw"], w["cab"], w["lag"], w["lab"], w["cbw"], mkT, mv, tril, triu)


def _attn_kernel(qT_ref, k_ref, vT_ref, cq_ref, ck_ref, sg_ref, o_ref, *, tile):
    i = pl.program_id(2)
    qT = qT_ref[...]
    feat_head = lax.broadcasted_iota(jnp.int32, qT.shape, 0) // HEAD_DIM
    n_pair = qT.shape[0] // HEAD_DIM
    qTm = [jnp.where(feat_head == h, qT, jnp.zeros_like(qT)) for h in range(n_pair)]
    cq = [cq_ref[h:h + 1, :] for h in range(n_pair)]
    k_pos = lax.broadcasted_iota(jnp.int32, (tile, tile), 0)
    q_pos = lax.broadcasted_iota(jnp.int32, (tile, tile), 1)
    causal = k_pos <= q_pos

    def step(j, carry, masked):
        start = pl.multiple_of(j * tile, tile)
        kblk = k_ref[pl.ds(start, tile), :]
        ck = ck_ref[pl.ds(start, tile), :]
        new = []
        for h in range(n_pair):
            m, l, acc = carry[h]
            s = _dot(kblk, qTm[h]) + (cq[h] - ck[:, h:h + 1])
            if masked:
                s = jnp.where(causal, s, NEG_BIG)
            m_new = jnp.maximum(m, jnp.max(s, axis=0, keepdims=True))
            p = jnp.exp(s - m_new)
            alpha = jnp.exp(m - m_new)
            l = alpha * l + jnp.sum(p, axis=0, keepdims=True)
            vblk = vT_ref[h * HEAD_DIM:(h + 1) * HEAD_DIM, pl.ds(start, tile)]
            acc = alpha * acc + _dot(vblk, p.astype(BF16))
            new.append((m_new, l, acc))
        return tuple(new)

    init = tuple((jnp.full((1, tile), NEG_BIG, F32), jnp.zeros((1, tile), F32),
                  jnp.zeros((HEAD_DIM, tile), F32)) for _ in range(n_pair))
    carry = lax.fori_loop(0, i, lambda j, c: step(j, c, False), init)
    carry = step(i, carry, True)
    oT = jnp.concatenate([acc / l for (_, l, acc) in carry], axis=0)
    o_ref[...] = (oT.T * sg_ref[...].astype(F32)).astype(BF16)


def _attn_call(qT, k, vT, cq, ck, sg):
    bsz, cw, seq = qT.shape
    tile = min(ATTN_TILE, seq)
    pair_w = 2 * HEAD_DIM
    n_pairs = cw // pair_w
    return pl.pallas_call(
        functools.partial(_attn_kernel, tile=tile),
        grid=(bsz, n_pairs, seq // tile),
        in_specs=[
            pl.BlockSpec((None, pair_w, tile), lambda b, p, i: (b, p, i)),
            pl.BlockSpec((None, seq, pair_w), lambda b, p, i: (b, 0, p)),
            pl.BlockSpec((None, pair_w, seq), lambda b, p, i: (b, p, 0)),
            pl.BlockSpec((None, None, 2, tile), lambda b, p, i: (b, p, 0, i)),
            pl.BlockSpec((None, None, seq, 2), lambda b, p, i: (b, p, 0, 0)),
            pl.BlockSpec((None, tile, pair_w), lambda b, p, i: (b, i, p)),
        ],
        out_specs=pl.BlockSpec((None, tile, pair_w), lambda b, p, i: (b, i, p)),
        out_shape=jax.ShapeDtypeStruct((bsz, seq, cw), BF16),
        compiler_params=pltpu.CompilerParams(dimension_semantics=("arbitrary", "arbitrary", "arbitrary"),
                                             vmem_limit_bytes=VMEM_LIMIT_BYTES),
        name="forget_attn",
    )(qT, k, vT, cq, ck, sg)


def _merge_kernel(x_ref, ha_ref, hb_ref, hc_ref, hm_ref, wg_ref, pa_ref, pb_ref, pc_ref, pm_ref,
                  wo_ref, g_ref, b_ref, o_ref, *, alpha):
    x = x_ref[...]
    xb = x.astype(BF16)
    d = x.shape[-1]
    merged = None
    for n, (h_ref, p_ref) in enumerate(((ha_ref, pa_ref), (hb_ref, pb_ref), (hc_ref, pc_ref), (hm_ref, pm_ref))):
        gate = _sigmoid(_dot(xb, wg_ref[:, n * d:(n + 1) * d]))
        term = gate * _dot(h_ref[...], p_ref[...])
        merged = term if merged is None else merged + term
    out = _dot(merged.astype(BF16), wo_ref[...])
    o_ref[...] = _layer_norm(alpha * x + out, g_ref[...], b_ref[...])


def _merge_call(x, ha, hb, hc, hm, layer, w, alpha):
    bsz, seq, d = x.shape
    tm = min(TOKEN_TILE, seq)
    row = lambda width: pl.BlockSpec((None, tm, width), lambda i, s: (i, s, 0))
    in_specs = [row(d), row(ha.shape[-1]), row(hb.shape[-1]), row(hc.shape[-1]), row(hm.shape[-1])]
    names = ("wg", "pa", "pb", "pc", "pm", "wo", "lng", "lnb")
    in_specs += [_const_spec(w[n].shape[1:], layer) for n in names]
    return pl.pallas_call(
        functools.partial(_merge_kernel, alpha=alpha),
        grid=(bsz, seq // tm), in_specs=in_specs, out_specs=row(d),
        out_shape=jax.ShapeDtypeStruct((bsz, seq, d), F32),
        compiler_params=pltpu.CompilerParams(dimension_semantics=("arbitrary", "arbitrary"),
                                             vmem_limit_bytes=VMEM_LIMIT_BYTES),
        name="merge_out_norm",
    )(x, ha, hb, hc, hm, *[w[n] for n in names])


def _prepare_weights(w_in, b_forget, conv_a_w, conv_a_b, ln_a_g, ln_a_b, conv_b_w, p_a, p_b, p_c, p_m,
                     w_out, ln_g, ln_b, dims):
    aw, bw, cw, mw, nch, d = dims["aw"], dims["bw"], dims["cw"], dims["mw"], dims["n_cheads"], dims["d"]
    depth = w_in.shape[0]
    cuts = [0]
    for width in (3 * aw, 4 * bw, cw, cw, cw, nch, cw, 2 * mw, 4 * d):
        cuts.append(cuts[-1] + width)
    assert cuts[-1] == w_in.shape[-1]
    seg = lambda n: w_in[:, :, cuts[n]:cuts[n + 1]]
    wq, wk, wv, wf, wcg = seg(2), seg(3), seg(4), seg(5), seg(6)
    wf_cols = jnp.pad(wf, ((0, 0), (0, 0), (0, LANES - nch)))
    wf_rows = jnp.pad(jnp.swapaxes(wf, 1, 2), ((0, 0), (0, BF16_ROWS - nch), (0, 0)))
    bf_pad_n = jnp.pad(b_forget, ((0, 0), (0, LANES - nch)))[:, None, :]
    bf_pad_t = jnp.broadcast_to(jnp.pad(b_forget, ((0, 0), (0, BF16_ROWS - nch)))[:, :, None],
                                (depth, BF16_ROWS, LANES))
    return {
        "wa": seg(0).astype(BF16), "wb": seg(1).astype(BF16),
        "wkc": jnp.concatenate([wk, wcg, wf_cols], axis=-1).astype(BF16),
        "wqvT": jnp.concatenate([jnp.swapaxes(wq, 1, 2), jnp.swapaxes(wv, 1, 2), wf_rows], axis=1).astype(BF16),
        "wm": seg(7).astype(BF16), "wg": seg(8).astype(BF16),
        "bfn": bf_pad_n, "bfT": bf_pad_t,
        "caw": conv_a_w, "cab": conv_a_b[:, None, :], "lag": ln_a_g[:, None, :], "lab": ln_a_b[:, None, :],
        "cbw": conv_b_w,
        "pa": p_a.astype(BF16), "pb": p_b.astype(BF16), "pc": p_c.astype(BF16), "pm": p_m.astype(BF16),
        "wo": w_out.astype(BF16), "lng": ln_g[:, None, :], "lnb": ln_b[:, None, :],
    }


def kernel(x, mem, w_in, b_forget, conv_a_w, conv_a_b, ln_a_g, ln_a_b, conv_b_w, w_kv_mem, mem_ln_g, mem_ln_b,
           p_a, p_b, p_c, p_m, w_out, ln_g, ln_b):
    bsz, seq, d = x.shape
    depth = w_in.shape[0]
    dims = {"d": d, "aw": conv_a_w.shape[-1], "bw": conv_b_w.shape[-1], "cw": p_c.shape[1], "mw": p_m.shape[1],
            "n_cheads": b_forget.shape[-1]}
    assert dims["cw"] == dims["n_cheads"] * HEAD_DIM and dims["n_cheads"] % 2 == 0
    assert dims["mw"] % HEAD_DIM == 0 and seq % min(TOKEN_TILE, seq) == 0 and seq % min(ATTN_TILE, seq) == 0
    assert conv_a_w.shape[1] - 1 <= A_HIST and conv_b_w.shape[1] - 1 <= B_HIST
    alpha = (2.0 * depth) ** 0.25
    w = _prepare_weights(w_in, b_forget, conv_a_w, conv_a_b, ln_a_g, ln_a_b, conv_b_w, p_a, p_b, p_c, p_m,
                         w_out, ln_g, ln_b, dims)
    mw = dims["mw"]
    wkT_mem = jnp.swapaxes(w_kv_mem[:, :, :mw], 1, 2).astype(BF16)
    wv_mem = w_kv_mem[:, :, mw:].astype(BF16)
    mkT, mv = _mem_call(mem, mem_ln_g[None, :], mem_ln_b[None, :], wkT_mem, wv_mem)
    tm = min(TOKEN_TILE, seq)
    tril = jnp.tril(jnp.ones((tm, tm), BF16))
    triu = jnp.triu(jnp.ones((tm, tm), BF16))
    n_pairs = dims["n_cheads"] // 2
    for layer in range(depth):
        ha, hb, k, sgc, qT, vT, cum, cumT, hm = _proj_call(x, layer, w, mkT, mv, tril, triu, dims)
        cq = cumT.reshape(bsz, n_pairs, 2, seq)
        ck = cum.reshape(bsz, seq, n_pairs, 2).transpose(0, 2, 1, 3)
        hc = _attn_call(qT, k, vT, cq, ck, sgc)
        x = _merge_call(x, ha, hb, hc, hm, layer, w, alpha)
    return x
```

```python
import functools

import jax
import jax.numpy as jnp
import numpy as np
from jax import lax
from jax.experimental import pallas as pl
from jax.experimental.pallas import tpu as pltpu

F32 = jnp.float32
BF16 = jnp.bfloat16

LN_EPS = 1e-5
HEAD_DIM = 64
LANES = 128
SUBLANES = 8
BF16_ROWS = 16
VMEM_LIMIT_BYTES = 56 * 1024 * 1024
NEG_BIG = -1e30
LOG2E = 1.4426950408889634

TOKEN_TILE = 512
ATTN_TILE = 256
CONV_ROW_CHUNK = 32
A_HIST = 32
B_HIST = 8


def _dot(a, b):
    return jnp.dot(a, b, preferred_element_type=F32)


def _dot_nt(a, b):
    return lax.dot_general(a, b, (((1,), (1,)), ((), ())), preferred_element_type=F32)


def _sigmoid(v):
    return 1.0 / (1.0 + jnp.exp(-v))


def _silu(v):
    return v * _sigmoid(v)


def _log_sigmoid(v):
    return jnp.minimum(v, 0.0) - jnp.log1p(jnp.exp(-jnp.abs(v)))


def _layer_norm(v, g, b):
    mu = jnp.mean(v, axis=-1, keepdims=True)
    c = v - mu
    var = jnp.mean(c * c, axis=-1, keepdims=True)
    return c * lax.rsqrt(var + LN_EPS) * g + b


def _split3(v):
    hi = v.astype(BF16)
    r1 = v - hi.astype(F32)
    mid = r1.astype(BF16)
    lo = (r1 - mid.astype(F32)).astype(BF16)
    return hi, mid, lo


def _const_spec(shape, layer):
    nd = len(shape)
    return pl.BlockSpec((None,) + tuple(shape), lambda *_, l=layer, nd=nd: (l,) + (0,) * nd,
                        pipeline_mode=pl.Buffered(1))


def _mem_kernel(mem_ref, g_ref, b_ref, wkT_ref, wv_ref, mkT_ref, mv_ref, *, n_heads):
    mem_n = _layer_norm(mem_ref[...], g_ref[...], b_ref[...]).astype(BF16)
    mkT = _dot_nt(wkT_ref[...], mem_n)
    mv = _dot(mem_n, wv_ref[...])
    row_head = lax.broadcasted_iota(jnp.int32, mkT.shape, 0) // HEAD_DIM
    col_head = lax.broadcasted_iota(jnp.int32, mv.shape, 1) // HEAD_DIM
    for h in range(n_heads):
        mkT_ref[h] = jnp.where(row_head == h, mkT, 0.0).astype(BF16)
        mv_ref[h] = jnp.where(col_head == h, mv, 0.0).astype(BF16)


def _mem_call(mem, g, b, wkT, wv):
    bsz, mlen, d = mem.shape
    depth, mw, _ = wkT.shape
    n_heads = mw // HEAD_DIM
    out = jax.ShapeDtypeStruct((bsz, depth, n_heads, mw, mlen), BF16), \
        jax.ShapeDtypeStruct((bsz, depth, n_heads, mlen, mw), BF16)
    return pl.pallas_call(
        functools.partial(_mem_kernel, n_heads=n_heads),
        grid=(bsz, depth),
        in_specs=[
            pl.BlockSpec((None, mlen, d), lambda i, l: (i, 0, 0)),
            pl.BlockSpec((1, d), lambda i, l: (0, 0)),
            pl.BlockSpec((1, d), lambda i, l: (0, 0)),
            pl.BlockSpec((None, mw, d), lambda i, l: (l, 0, 0)),
            pl.BlockSpec((None, d, mw), lambda i, l: (l, 0, 0)),
        ],
        out_specs=[
            pl.BlockSpec((None, None, n_heads, mw, mlen), lambda i, l: (i, l, 0, 0, 0)),
            pl.BlockSpec((None, None, n_heads, mlen, mw), lambda i, l: (i, l, 0, 0, 0)),
        ],
        out_shape=out,
        compiler_params=pltpu.CompilerParams(dimension_semantics=("arbitrary", "arbitrary"),
                                             vmem_limit_bytes=VMEM_LIMIT_BYTES),
        name="mem_kv",
    )(mem, g, b, wkT, wv)


def _proj_kernel(x_ref, wa_ref, wb_ref, wkc_ref, wqvT_ref, wm_ref, bfn_ref, bfT_ref,
                 caw_ref, cab_ref, lag_ref, lab_ref, cbw_ref, mkT_ref, mv_ref, tril_ref, triu_ref,
                 selk_ref, selq_ref,
                 ha_ref, hb_ref, ke_ref, sgc_ref, qte_ref, vT_ref, hm_ref,
                 abuf, ashift, aconv, bbuf, carry_n, carry_t,
                 *, tm, aw, bw, cw, mw, n_cheads, taps_a, taps_b):
    @pl.when(pl.program_id(1) == 0)
    def _():
        abuf[0:A_HIST, :] = jnp.zeros((A_HIST, aw), F32)
        bbuf[0:B_HIST, :] = jnp.zeros((B_HIST, bw), F32)
        carry_n[...] = jnp.zeros_like(carry_n)
        carry_t[...] = jnp.zeros_like(carry_t)

    xb = x_ref[...].astype(BF16)

    xa = _dot(xb, wa_ref[...])
    abuf[A_HIST:A_HIST + tm, :] = xa[:, :aw] * _sigmoid(xa[:, aw:2 * aw])
    n_shift_rows = tm + A_HIST - SUBLANES
    for sh in range(1, SUBLANES):
        ashift[sh - 1] = abuf[sh:sh + n_shift_rows, :]
    first = A_HIST - (taps_a - 1)

    def conv_chunk(c, carry):
        r0 = pl.multiple_of(c * CONV_ROW_CHUNK, CONV_ROW_CHUNK)
        acc = jnp.broadcast_to(cab_ref[...], (CONV_ROW_CHUNK, aw))
        for t in range(taps_a):
            blk, sh = divmod(first + t, SUBLANES)
            start = r0 + blk * SUBLANES
            if sh == 0:
                rows = abuf[pl.ds(start, CONV_ROW_CHUNK), :]
            else:
                rows = ashift[sh - 1, pl.ds(start, CONV_ROW_CHUNK), :]
            acc = acc + caw_ref[t:t + 1, :] * rows
        aconv[pl.ds(r0, CONV_ROW_CHUNK), :] = acc
        return carry

    lax.fori_loop(0, tm // CONV_ROW_CHUNK, conv_chunk, 0)
    abuf[0:A_HIST, :] = abuf[tm:tm + A_HIST, :]
    a = _silu(_layer_norm(aconv[...], lag_ref[...], lab_ref[...]))
    ha_ref[...] = (a * _silu(xa[:, 2 * aw:])).astype(BF16)

    xbm = _dot(xb, wb_ref[...])
    bbuf[B_HIST:B_HIST + tm, :] = xbm[:, 2 * bw:3 * bw] * xbm[:, :bw]
    firstb = B_HIST - (taps_b - 1)
    convb = cbw_ref[0:1, :] * bbuf[firstb:firstb + tm, :]
    for t in range(1, taps_b):
        convb = convb + cbw_ref[t:t + 1, :] * bbuf[firstb + t:firstb + t + tm, :]
    bbuf[0:B_HIST, :] = bbuf[tm:tm + B_HIST, :]
    hb_ref[...] = (xbm[:, bw:2 * bw] * convb * _silu(xbm[:, 3 * bw:])).astype(BF16)

    xk = _dot(xb, wkc_ref[...])
    sgc_ref[...] = _silu(xk[:, cw:2 * cw]).astype(BF16)
    lf = _log_sigmoid(xk[:, 2 * cw:] + bfn_ref[...])
    hi, mid, lo = _split3(lf)
    tril = tril_ref[...]
    cum = _dot(tril, hi) + _dot(tril, mid) + _dot(tril, lo) + carry_n[...]
    carry_n[...] = cum[tm - 1:tm, :]
    parts = jnp.concatenate(_split3(cum * LOG2E), axis=1)
    lane = lax.broadcasted_iota(jnp.int32, (tm, LANES), 1)
    ones_cols = jnp.where((lane >= 3 * n_cheads) & (lane < 6 * n_cheads), 1.0, 0.0)
    ke_ref[:, :cw] = xk[:, :cw].astype(BF16)
    ke_ref[:, cw:] = (_dot(parts, selk_ref[...]) + ones_cols).astype(BF16)

    qv = _dot_nt(wqvT_ref[...], xb)
    vT_ref[...] = qv[cw:2 * cw].astype(BF16)
    lft = _log_sigmoid(qv[2 * cw:] + bfT_ref[...][:, 0:1])
    hi, mid, lo = _split3(lft)
    triu = triu_ref[...]
    cumt = _dot(hi, triu) + _dot(mid, triu) + _dot(lo, triu) + carry_t[...][:, 0:1]
    carry_t[...] = jnp.broadcast_to(cumt[:, tm - 1:tm], carry_t.shape)
    parts_t = jnp.concatenate(_split3(cumt * LOG2E), axis=0)
    sub = lax.broadcasted_iota(jnp.int32, (LANES, tm), 0)
    ones_rows = jnp.where(sub < 3 * n_cheads, 1.0, 0.0)
    qte_ref[:cw, :] = (qv[:cw] * (LOG2E * HEAD_DIM ** -0.5)).astype(BF16)
    qte_ref[cw:, :] = (_dot(selq_ref[...], parts_t) + ones_rows).astype(BF16)

    xm = _dot(xb, wm_ref[...])
    mq = (xm[:, :mw] * (HEAD_DIM ** -0.5)).astype(BF16)
    om = jnp.zeros((tm, mw), F32)
    for h in range(mw // HEAD_DIM):
        s = _dot(mq, mkT_ref[h])
        e = jnp.exp(s - jnp.max(s, axis=-1, keepdims=True))
        p = e / jnp.sum(e, axis=-1, keepdims=True)
        om = om + _dot(p.astype(BF16), mv_ref[h])
    hm_ref[...] = (om * _silu(xm[:, mw:])).astype(BF16)


def _proj_call(x, layer, w, mkT, mv, tril, triu, selk, selq, dims):
    bsz, seq, d = x.shape
    tm = min(TOKEN_TILE, seq)
    aw, bw, cw, mw, nch = dims["aw"], dims["bw"], dims["cw"], dims["mw"], dims["n_cheads"]
    taps_a, taps_b = w["caw"].shape[1], w["cbw"].shape[1]
    n_heads_m = mw // HEAD_DIM
    mlen = mv.shape[3]
    row = lambda width: pl.BlockSpec((None, tm, width), lambda i, s: (i, s, 0))
    col = lambda height: pl.BlockSpec((None, height, tm), lambda i, s: (i, 0, s))
    in_specs = [
        row(d),
        _const_spec(w["wa"].shape[1:], layer), _const_spec(w["wb"].shape[1:], layer),
        _const_spec(w["wkc"].shape[1:], layer), _const_spec(w["wqvT"].shape[1:], layer),
        _const_spec(w["wm"].shape[1:], layer),
        _const_spec(w["bfn"].shape[1:], layer), _const_spec(w["bfT"].shape[1:], layer),
        _const_spec(w["caw"].shape[1:], layer), _const_spec(w["cab"].shape[1:], layer),
        _const_spec(w["lag"].shape[1:], layer), _const_spec(w["lab"].shape[1:], layer),
        _const_spec(w["cbw"].shape[1:], layer),
        pl.BlockSpec((None, None, n_heads_m, mw, mlen), lambda i, s, l=layer: (i, l, 0, 0, 0)),
        pl.BlockSpec((None, None, n_heads_m, mlen, mw), lambda i, s, l=layer: (i, l, 0, 0, 0)),
        pl.BlockSpec((tm, tm), lambda i, s: (0, 0), pipeline_mode=pl.Buffered(1)),
        pl.BlockSpec((tm, tm), lambda i, s: (0, 0), pipeline_mode=pl.Buffered(1)),
        pl.BlockSpec(selk.shape, lambda i, s: (0, 0), pipeline_mode=pl.Buffered(1)),
        pl.BlockSpec(selq.shape, lambda i, s: (0, 0), pipeline_mode=pl.Buffered(1)),
    ]
    out_specs = [row(aw), row(bw), row(cw + LANES), row(cw), col(cw + LANES), col(cw), row(mw)]
    out_shape = [
        jax.ShapeDtypeStruct((bsz, seq, aw), BF16), jax.ShapeDtypeStruct((bsz, seq, bw), BF16),
        jax.ShapeDtypeStruct((bsz, seq, cw + LANES), BF16), jax.ShapeDtypeStruct((bsz, seq, cw), BF16),
        jax.ShapeDtypeStruct((bsz, cw + LANES, seq), BF16), jax.ShapeDtypeStruct((bsz, cw, seq), BF16),
        jax.ShapeDtypeStruct((bsz, seq, mw), BF16),
    ]
    scratch = [
        pltpu.VMEM((A_HIST + tm, aw), F32),
        pltpu.VMEM((SUBLANES - 1, tm + A_HIST - SUBLANES, aw), F32),
        pltpu.VMEM((tm, aw), F32),
        pltpu.VMEM((B_HIST + tm, bw), F32),
        pltpu.VMEM((1, LANES), F32),
        pltpu.VMEM((BF16_ROWS, LANES), F32),
    ]
    kern = functools.partial(_proj_kernel, tm=tm, aw=aw, bw=bw, cw=cw, mw=mw, n_cheads=nch,
                             taps_a=taps_a, taps_b=taps_b)
    return pl.pallas_call(
        kern, grid=(bsz, seq // tm), in_specs=in_specs, out_specs=out_specs, out_shape=out_shape,
        scratch_shapes=scratch,
        compiler_params=pltpu.CompilerParams(dimension_semantics=("arbitrary", "arbitrary"),
                                             vmem_limit_bytes=VMEM_LIMIT_BYTES),
        name="proj_branches",
    )(x, w["wa"], w["wb"], w["wkc"], w["wqvT"], w["wm"], w["bfn"], w["bfT"],
      w["caw"], w["cab"], w["lag"], w["lab"], w["cbw"], mkT, mv, tril, triu, selk, selq)


def _attn_kernel(qte_ref, ke_ref, vT_ref, sg_ref, o_ref, qh_ref, m_ref, acc_ref, sa_ref, sb_ref,
                 *, tile, cw, n_heads):
    i = pl.program_id(1)
    pair_w = 2 * HEAD_DIM
    extra = qte_ref[cw:, :]
    extra_row = lax.broadcasted_iota(jnp.int32, extra.shape, 0)
    feat_row = lax.broadcasted_iota(jnp.int32, (pair_w, tile), 0)
    for h in range(n_heads):
        pair, hh = divmod(h, 2)
        qpair = qte_ref[pair * pair_w:(pair + 1) * pair_w, :]
        own = (feat_row >= hh * HEAD_DIM) & (feat_row < (hh + 1) * HEAD_DIM)
        qm = jnp.where(own, qpair, jnp.zeros_like(qpair))
        k_rows, q_rows = 3 * h, 3 * (n_heads + h)
        own = (((extra_row >= k_rows) & (extra_row < k_rows + 3))
               | ((extra_row >= q_rows) & (extra_row < q_rows + 3)))
        em = jnp.where(own, extra, jnp.zeros_like(extra))
        qh_ref[h] = jnp.concatenate([qm, em], axis=0)
    m_ref[...] = jnp.full(m_ref.shape, NEG_BIG, F32)
    acc_ref[...] = jnp.zeros(acc_ref.shape, F32)
    ones_rows = jnp.ones((BF16_ROWS, tile), BF16)
    causal = (lax.broadcasted_iota(jnp.int32, (tile, tile), 0)
              <= lax.broadcasted_iota(jnp.int32, (tile, tile), 1))

    def scores_into(s_ref, j):
        start = pl.multiple_of(j * tile, tile)
        kx = ke_ref[pl.ds(start, tile), cw:]
        for h in range(n_heads):
            pair = h // 2
            kblk = jnp.concatenate([ke_ref[pl.ds(start, tile), pair * pair_w:(pair + 1) * pair_w], kx], axis=1)
            s_ref[h] = _dot(kblk, qh_ref[h])

    def absorb(s_ref, j, masked):
        start = pl.multiple_of(j * tile, tile)
        for h in range(n_heads):
            s = s_ref[h]
            if masked:
                s = jnp.where(causal, s, NEG_BIG)
            m_old = m_ref[h]
            m_new = jnp.maximum(m_old, jnp.max(s, axis=0, keepdims=True))
            p = jnp.exp2(s - m_new)
            alpha = jnp.exp2(m_old - m_new)
            m_ref[h] = m_new
            vblk = jnp.concatenate([vT_ref[h * HEAD_DIM:(h + 1) * HEAD_DIM, pl.ds(start, tile)], ones_rows], axis=0)
            acc_ref[h] = alpha * acc_ref[h] + _dot(vblk, p.astype(BF16))

    scores_into(sa_ref, 0)

    def two_blocks(jj, carry):
        j = 2 * jj
        scores_into(sb_ref, j + 1)
        absorb(sa_ref, j, False)
        scores_into(sa_ref, j + 2)
        absorb(sb_ref, j + 1, False)
        return carry

    lax.fori_loop(0, i // 2, two_blocks, 0)

    @pl.when(i % 2 == 0)
    def _():
        absorb(sa_ref, i, True)

    @pl.when(i % 2 == 1)
    def _():
        scores_into(sb_ref, i)
        absorb(sa_ref, i - 1, False)
        absorb(sb_ref, i, True)

    for pair in range(n_heads // 2):
        oT = jnp.concatenate([acc_ref[h, :HEAD_DIM, :] / acc_ref[h, HEAD_DIM:HEAD_DIM + 1, :]
                              for h in (2 * pair, 2 * pair + 1)], axis=0)
        cols = slice(pair * pair_w, (pair + 1) * pair_w)
        o_ref[:, cols] = (oT.T * sg_ref[:, cols].astype(F32)).astype(BF16)


def _attn_call(qte, ke, vT, sg):
    bsz, cw, seq = vT.shape
    tile = min(ATTN_TILE, seq)
    n_heads = cw // HEAD_DIM
    kw = cw + LANES
    return pl.pallas_call(
        functools.partial(_attn_kernel, tile=tile, cw=cw, n_heads=n_heads),
        grid=(bsz, seq // tile),
        in_specs=[
            pl.BlockSpec((None, kw, tile), lambda b, i: (b, 0, i)),
            pl.BlockSpec((None, seq, kw), lambda b, i: (b, 0, 0)),
            pl.BlockSpec((None, cw, seq), lambda b, i: (b, 0, 0)),
            pl.BlockSpec((None, tile, cw), lambda b, i: (b, i, 0)),
        ],
        out_specs=pl.BlockSpec((None, tile, cw), lambda b, i: (b, i, 0)),
        out_shape=jax.ShapeDtypeStruct((bsz, seq, cw), BF16),
        scratch_shapes=[
            pltpu.VMEM((n_heads, 2 * HEAD_DIM + LANES, tile), BF16),
            pltpu.VMEM((n_heads, 1, tile), F32),
            pltpu.VMEM((n_heads, HEAD_DIM + BF16_ROWS, tile), F32),
            pltpu.VMEM((n_heads, tile, tile), F32),
            pltpu.VMEM((n_heads, tile, tile), F32),
        ],
        compiler_params=pltpu.CompilerParams(dimension_semantics=("arbitrary", "arbitrary"),
                                             vmem_limit_bytes=VMEM_LIMIT_BYTES),
        name="forget_attn",
    )(qte, ke, vT, sg)


def _merge_kernel(x_ref, ha_ref, hb_ref, hc_ref, hm_ref, wg_ref, pa_ref, pb_ref, pc_ref, pm_ref,
                  wo_ref, g_ref, b_ref, o_ref, *, alpha):
    x = x_ref[...]
    xb = x.astype(BF16)
    d = x.shape[-1]
    merged = None
    for n, (h_ref, p_ref) in enumerate(((ha_ref, pa_ref), (hb_ref, pb_ref), (hc_ref, pc_ref), (hm_ref, pm_ref))):
        gate = _sigmoid(_dot(xb, wg_ref[:, n * d:(n + 1) * d]))
        term = gate * _dot(h_ref[...], p_ref[...])
        merged = term if merged is None else merged + term
    out = _dot(merged.astype(BF16), wo_ref[...])
    o_ref[...] = _layer_norm(alpha * x + out, g_ref[...], b_ref[...])


def _merge_call(x, ha, hb, hc, hm, layer, w, alpha):
    bsz, seq, d = x.shape
    tm = min(TOKEN_TILE, seq)
    row = lambda width: pl.BlockSpec((None, tm, width), lambda i, s: (i, s, 0))
    in_specs = [row(d), row(ha.shape[-1]), row(hb.shape[-1]), row(hc.shape[-1]), row(hm.shape[-1])]
    names = ("wg", "pa", "pb", "pc", "pm", "wo", "lng", "lnb")
    in_specs += [_const_spec(w[n].shape[1:], layer) for n in names]
    return pl.pallas_call(
        functools.partial(_merge_kernel, alpha=alpha),
        grid=(bsz, seq // tm), in_specs=in_specs, out_specs=row(d),
        out_shape=jax.ShapeDtypeStruct((bsz, seq, d), F32),
        compiler_params=pltpu.CompilerParams(dimension_semantics=("arbitrary", "arbitrary"),
                                             vmem_limit_bytes=VMEM_LIMIT_BYTES),
        name="merge_out_norm",
    )(x, ha, hb, hc, hm, *[w[n] for n in names])


def _prepare_weights(w_in, b_forget, conv_a_w, conv_a_b, ln_a_g, ln_a_b, conv_b_w, p_a, p_b, p_c, p_m,
                     w_out, ln_g, ln_b, dims):
    aw, bw, cw, mw, nch, d = dims["aw"], dims["bw"], dims["cw"], dims["mw"], dims["n_cheads"], dims["d"]
    depth = w_in.shape[0]
    cuts = [0]
    for width in (3 * aw, 4 * bw, cw, cw, cw, nch, cw, 2 * mw, 4 * d):
        cuts.append(cuts[-1] + width)
    assert cuts[-1] == w_in.shape[-1]
    seg = lambda n: w_in[:, :, cuts[n]:cuts[n + 1]]
    wq, wk, wv, wf, wcg = seg(2), seg(3), seg(4), seg(5), seg(6)
    wf_cols = jnp.pad(wf, ((0, 0), (0, 0), (0, LANES - nch)))
    wf_rows = jnp.pad(jnp.swapaxes(wf, 1, 2), ((0, 0), (0, BF16_ROWS - nch), (0, 0)))
    bf_pad_n = jnp.pad(b_forget, ((0, 0), (0, LANES - nch)))[:, None, :]
    bf_pad_t = jnp.broadcast_to(jnp.pad(b_forget, ((0, 0), (0, BF16_ROWS - nch)))[:, :, None],
                                (depth, BF16_ROWS, LANES))
    return {
        "wa": seg(0).astype(BF16), "wb": seg(1).astype(BF16),
        "wkc": jnp.concatenate([wk, wcg, wf_cols], axis=-1).astype(BF16),
        "wqvT": jnp.concatenate([jnp.swapaxes(wq, 1, 2), jnp.swapaxes(wv, 1, 2), wf_rows], axis=1).astype(BF16),
        "wm": seg(7).astype(BF16), "wg": seg(8).astype(BF16),
        "bfn": bf_pad_n, "bfT": bf_pad_t,
        "caw": conv_a_w, "cab": conv_a_b[:, None, :], "lag": ln_a_g[:, None, :], "lab": ln_a_b[:, None, :],
        "cbw": conv_b_w,
        "pa": p_a.astype(BF16), "pb": p_b.astype(BF16), "pc": p_c.astype(BF16), "pm": p_m.astype(BF16),
        "wo": w_out.astype(BF16), "lng": ln_g[:, None, :], "lnb": ln_b[:, None, :],
    }


def _decay_selectors(n_heads):
    assert 6 * n_heads <= LANES and n_heads <= BF16_ROWS
    selk = np.zeros((3 * LANES, LANES), np.float32)
    selq = np.zeros((LANES, 3 * BF16_ROWS), np.float32)
    for h in range(n_heads):
        for r in range(3):
            selk[r * LANES + h, 3 * h + r] = -1.0
            selq[3 * (n_heads + h) + r, r * BF16_ROWS + h] = 1.0
    return jnp.asarray(selk, BF16), jnp.asarray(selq, BF16)


def kernel(x, mem, w_in, b_forget, conv_a_w, conv_a_b, ln_a_g, ln_a_b, conv_b_w, w_kv_mem, mem_ln_g, mem_ln_b,
           p_a, p_b, p_c, p_m, w_out, ln_g, ln_b):
    bsz, seq, d = x.shape
    depth = w_in.shape[0]
    dims = {"d": d, "aw": conv_a_w.shape[-1], "bw": conv_b_w.shape[-1], "cw": p_c.shape[1], "mw": p_m.shape[1],
            "n_cheads": b_forget.shape[-1]}
    assert dims["cw"] == dims["n_cheads"] * HEAD_DIM and dims["n_cheads"] % 2 == 0
    assert dims["mw"] % HEAD_DIM == 0 and seq % min(TOKEN_TILE, seq) == 0 and seq % min(ATTN_TILE, seq) == 0
    assert conv_a_w.shape[1] - 1 <= A_HIST and conv_b_w.shape[1] - 1 <= B_HIST
    alpha = (2.0 * depth) ** 0.25
    w = _prepare_weights(w_in, b_forget, conv_a_w, conv_a_b, ln_a_g, ln_a_b, conv_b_w, p_a, p_b, p_c, p_m,
                         w_out, ln_g, ln_b, dims)
    mw = dims["mw"]
    wkT_mem = jnp.swapaxes(w_kv_mem[:, :, :mw], 1, 2).astype(BF16)
    wv_mem = w_kv_mem[:, :, mw:].astype(BF16)
    mkT, mv = _mem_call(mem, mem_ln_g[None, :], mem_ln_b[None, :], wkT_mem, wv_mem)
    tm = min(TOKEN_TILE, seq)
    tril = jnp.tril(jnp.ones((tm, tm), BF16))
    triu = jnp.triu(jnp.ones((tm, tm), BF16))
    selk, selq = _decay_selectors(dims["n_cheads"])
    for layer in range(depth):
        ha, hb, ke, sgc, qte, vT, hm = _proj_call(x, layer, w, mkT, mv, tril, triu, selk, selq, dims)
        hc = _attn_call(qte, ke, vT, sgc)
        x = _merge_call(x, ha, hb, hc, hm, layer, w, alpha)
    return x
```

```python
import functools

import jax
import jax.numpy as jnp
import numpy as np
from jax import lax
from jax.experimental import pallas as pl
from jax.experimental.pallas import tpu as pltpu

F32 = jnp.float32
BF16 = jnp.bfloat16

LN_EPS = 1e-5
HEAD_DIM = 64
LANES = 128
SUBLANES = 8
BF16_ROWS = 16
VMEM_LIMIT_BYTES = 56 * 1024 * 1024
NEG_BIG = -1e30
LOG2E = 1.4426950408889634

TOKEN_TILE = 512
ATTN_TILE = 256
ATTN_KEY_TILE = 512
CONV_ROW_CHUNK = 32
A_HIST = 32
B_HIST = 8


def _dot(a, b):
    return jnp.dot(a, b, preferred_element_type=F32)


def _dot_nt(a, b):
    return lax.dot_general(a, b, (((1,), (1,)), ((), ())), preferred_element_type=F32)


def _sigmoid(v):
    return 1.0 / (1.0 + jnp.exp(-v))


def _silu(v):
    return v * _sigmoid(v)


def _log_sigmoid(v):
    return jnp.minimum(v, 0.0) - jnp.log1p(jnp.exp(-jnp.abs(v)))


def _layer_norm(v, g, b):
    mu = jnp.mean(v, axis=-1, keepdims=True)
    c = v - mu
    var = jnp.mean(c * c, axis=-1, keepdims=True)
    return c * lax.rsqrt(var + LN_EPS) * g + b


def _split3(v):
    hi = v.astype(BF16)
    r1 = v - hi.astype(F32)
    mid = r1.astype(BF16)
    lo = (r1 - mid.astype(F32)).astype(BF16)
    return hi, mid, lo


def _const_spec(shape, layer):
    nd = len(shape)
    return pl.BlockSpec((None,) + tuple(shape), lambda *_, l=layer, nd=nd: (l,) + (0,) * nd,
                        pipeline_mode=pl.Buffered(1))


def _mem_kernel(mem_ref, g_ref, b_ref, wkT_ref, wv_ref, mkT_ref, mv_ref, *, n_heads):
    mem_n = _layer_norm(mem_ref[...], g_ref[...], b_ref[...]).astype(BF16)
    mkT = _dot_nt(wkT_ref[...], mem_n)
    mv = _dot(mem_n, wv_ref[...])
    row_head = lax.broadcasted_iota(jnp.int32, mkT.shape, 0) // HEAD_DIM
    col_head = lax.broadcasted_iota(jnp.int32, mv.shape, 1) // HEAD_DIM
    for h in range(n_heads):
        mkT_ref[h] = jnp.where(row_head == h, mkT, 0.0).astype(BF16)
        mv_ref[h] = jnp.where(col_head == h, mv, 0.0).astype(BF16)


def _mem_call(mem, g, b, wkT, wv):
    bsz, mlen, d = mem.shape
    depth, mw, _ = wkT.shape
    n_heads = mw // HEAD_DIM
    out = jax.ShapeDtypeStruct((bsz, depth, n_heads, mw, mlen), BF16), \
        jax.ShapeDtypeStruct((bsz, depth, n_heads, mlen, mw), BF16)
    return pl.pallas_call(
        functools.partial(_mem_kernel, n_heads=n_heads),
        grid=(bsz, depth),
        in_specs=[
            pl.BlockSpec((None, mlen, d), lambda i, l: (i, 0, 0)),
            pl.BlockSpec((1, d), lambda i, l: (0, 0)),
            pl.BlockSpec((1, d), lambda i, l: (0, 0)),
            pl.BlockSpec((None, mw, d), lambda i, l: (l, 0, 0)),
            pl.BlockSpec((None, d, mw), lambda i, l: (l, 0, 0)),
        ],
        out_specs=[
            pl.BlockSpec((None, None, n_heads, mw, mlen), lambda i, l: (i, l, 0, 0, 0)),
            pl.BlockSpec((None, None, n_heads, mlen, mw), lambda i, l: (i, l, 0, 0, 0)),
        ],
        out_shape=out,
        compiler_params=pltpu.CompilerParams(dimension_semantics=("arbitrary", "arbitrary"),
                                             vmem_limit_bytes=VMEM_LIMIT_BYTES),
        name="mem_kv",
    )(mem, g, b, wkT, wv)


def _proj_kernel(x_ref, wa_ref, wb_ref, wkc_ref, wqvT_ref, wm_ref, bfn_ref, bfT_ref,
                 caw_ref, cab_ref, lag_ref, lab_ref, cbw_ref, mkT_ref, mv_ref, tril_ref, triu_ref,
                 selk_ref, selq_ref,
                 ha_ref, hb_ref, ke_ref, sgc_ref, qte_ref, vT_ref, hm_ref,
                 abuf, ashift, aconv, bbuf, carry_n, carry_t,
                 *, tm, aw, bw, cw, mw, n_cheads, taps_a, taps_b):
    @pl.when(pl.program_id(1) == 0)
    def _():
        abuf[0:A_HIST, :] = jnp.zeros((A_HIST, aw), F32)
        bbuf[0:B_HIST, :] = jnp.zeros((B_HIST, bw), F32)
        carry_n[...] = jnp.zeros_like(carry_n)
        carry_t[...] = jnp.zeros_like(carry_t)

    xb = x_ref[...].astype(BF16)

    xm = _dot(xb, wm_ref[...])
    xa = _dot(xb, wa_ref[...])
    mq = (xm[:, :mw] * (HEAD_DIM ** -0.5)).astype(BF16)
    mem_scores = [_dot(mq, mkT_ref[h]) for h in range(mw // HEAD_DIM)]
    xbm = _dot(xb, wb_ref[...])
    xk = _dot(xb, wkc_ref[...])
    qv = _dot_nt(wqvT_ref[...], xb)

    abuf[A_HIST:A_HIST + tm, :] = xa[:, :aw] * _sigmoid(xa[:, aw:2 * aw])
    n_shift_rows = tm + A_HIST - SUBLANES
    for sh in range(1, SUBLANES):
        ashift[sh - 1] = abuf[sh:sh + n_shift_rows, :]
    first = A_HIST - (taps_a - 1)
    for r0 in range(0, tm, CONV_ROW_CHUNK):
        acc = jnp.broadcast_to(cab_ref[...], (CONV_ROW_CHUNK, aw))
        for t in range(taps_a):
            blk, sh = divmod(first + t, SUBLANES)
            start = r0 + blk * SUBLANES
            if sh == 0:
                rows = abuf[start:start + CONV_ROW_CHUNK, :]
            else:
                rows = ashift[sh - 1, start:start + CONV_ROW_CHUNK, :]
            acc = acc + caw_ref[t:t + 1, :] * rows
        aconv[r0:r0 + CONV_ROW_CHUNK, :] = acc
    abuf[0:A_HIST, :] = abuf[tm:tm + A_HIST, :]
    a = _silu(_layer_norm(aconv[...], lag_ref[...], lab_ref[...]))
    ha_ref[...] = (a * _silu(xa[:, 2 * aw:])).astype(BF16)

    om = jnp.zeros((tm, mw), F32)
    for h, s in enumerate(mem_scores):
        e = jnp.exp(s - jnp.max(s, axis=-1, keepdims=True))
        p = e / jnp.sum(e, axis=-1, keepdims=True)
        om = om + _dot(p.astype(BF16), mv_ref[h])
    hm_ref[...] = (om * _silu(xm[:, mw:])).astype(BF16)

    bbuf[B_HIST:B_HIST + tm, :] = xbm[:, 2 * bw:3 * bw] * xbm[:, :bw]
    firstb = B_HIST - (taps_b - 1)
    convb = cbw_ref[0:1, :] * bbuf[firstb:firstb + tm, :]
    for t in range(1, taps_b):
        convb = convb + cbw_ref[t:t + 1, :] * bbuf[firstb + t:firstb + t + tm, :]
    bbuf[0:B_HIST, :] = bbuf[tm:tm + B_HIST, :]
    hb_ref[...] = (xbm[:, bw:2 * bw] * convb * _silu(xbm[:, 3 * bw:])).astype(BF16)

    sgc_ref[...] = _silu(xk[:, cw:2 * cw]).astype(BF16)
    lf = _log_sigmoid(xk[:, 2 * cw:] + bfn_ref[...])
    hi, mid, lo = _split3(lf)
    tril = tril_ref[...]
    cum = _dot(tril, hi) + _dot(tril, mid) + _dot(tril, lo) + carry_n[...]
    carry_n[...] = cum[tm - 1:tm, :]
    parts = jnp.concatenate(_split3(cum * LOG2E), axis=1)
    lane = lax.broadcasted_iota(jnp.int32, (tm, LANES), 1)
    ones_cols = jnp.where((lane >= 3 * n_cheads) & (lane < 6 * n_cheads), 1.0, 0.0)
    ke_ref[:, :cw] = xk[:, :cw].astype(BF16)
    ke_ref[:, cw:] = (_dot(parts, selk_ref[...]) + ones_cols).astype(BF16)

    vT_ref[...] = qv[cw:2 * cw].astype(BF16)
    lft = _log_sigmoid(qv[2 * cw:] + bfT_ref[...][:, 0:1])
    hi, mid, lo = _split3(lft)
    triu = triu_ref[...]
    cumt = _dot(hi, triu) + _dot(mid, triu) + _dot(lo, triu) + carry_t[...][:, 0:1]
    carry_t[...] = jnp.broadcast_to(cumt[:, tm - 1:tm], carry_t.shape)
    parts_t = jnp.concatenate(_split3(cumt * LOG2E), axis=0)
    sub = lax.broadcasted_iota(jnp.int32, (LANES, tm), 0)
    ones_rows = jnp.where(sub < 3 * n_cheads, 1.0, 0.0)
    qte_ref[:cw, :] = (qv[:cw] * (LOG2E * HEAD_DIM ** -0.5)).astype(BF16)
    qte_ref[cw:, :] = (_dot(selq_ref[...], parts_t) + ones_rows).astype(BF16)


def _proj_call(x, layer, w, mkT, mv, tril, triu, selk, selq, dims):
    bsz, seq, d = x.shape
    tm = min(TOKEN_TILE, seq)
    aw, bw, cw, mw, nch = dims["aw"], dims["bw"], dims["cw"], dims["mw"], dims["n_cheads"]
    taps_a, taps_b = w["caw"].shape[1], w["cbw"].shape[1]
    n_heads_m = mw // HEAD_DIM
    mlen = mv.shape[3]
    row = lambda width: pl.BlockSpec((None, tm, width), lambda i, s: (i, s, 0))
    col = lambda height: pl.BlockSpec((None, height, tm), lambda i, s: (i, 0, s))
    in_specs = [
        row(d),
        _const_spec(w["wa"].shape[1:], layer), _const_spec(w["wb"].shape[1:], layer),
        _const_spec(w["wkc"].shape[1:], layer), _const_spec(w["wqvT"].shape[1:], layer),
        _const_spec(w["wm"].shape[1:], layer),
        _const_spec(w["bfn"].shape[1:], layer), _const_spec(w["bfT"].shape[1:], layer),
        _const_spec(w["caw"].shape[1:], layer), _const_spec(w["cab"].shape[1:], layer),
        _const_spec(w["lag"].shape[1:], layer), _const_spec(w["lab"].shape[1:], layer),
        _const_spec(w["cbw"].shape[1:], layer),
        pl.BlockSpec((None, None, n_heads_m, mw, mlen), lambda i, s, l=layer: (i, l, 0, 0, 0)),
        pl.BlockSpec((None, None, n_heads_m, mlen, mw), lambda i, s, l=layer: (i, l, 0, 0, 0)),
        pl.BlockSpec((tm, tm), lambda i, s: (0, 0), pipeline_mode=pl.Buffered(1)),
        pl.BlockSpec((tm, tm), lambda i, s: (0, 0), pipeline_mode=pl.Buffered(1)),
        pl.BlockSpec(selk.shape, lambda i, s: (0, 0), pipeline_mode=pl.Buffered(1)),
        pl.BlockSpec(selq.shape, lambda i, s: (0, 0), pipeline_mode=pl.Buffered(1)),
    ]
    out_specs = [row(aw), row(bw), row(cw + LANES), row(cw), col(cw + LANES), col(cw), row(mw)]
    out_shape = [
        jax.ShapeDtypeStruct((bsz, seq, aw), BF16), jax.ShapeDtypeStruct((bsz, seq, bw), BF16),
        jax.ShapeDtypeStruct((bsz, seq, cw + LANES), BF16), jax.ShapeDtypeStruct((bsz, seq, cw), BF16),
        jax.ShapeDtypeStruct((bsz, cw + LANES, seq), BF16), jax.ShapeDtypeStruct((bsz, cw, seq), BF16),
        jax.ShapeDtypeStruct((bsz, seq, mw), BF16),
    ]
    scratch = [
        pltpu.VMEM((A_HIST + tm, aw), F32),
        pltpu.VMEM((SUBLANES - 1, tm + A_HIST - SUBLANES, aw), F32),
        pltpu.VMEM((tm, aw), F32),
        pltpu.VMEM((B_HIST + tm, bw), F32),
        pltpu.VMEM((1, LANES), F32),
        pltpu.VMEM((BF16_ROWS, LANES), F32),
    ]
    kern = functools.partial(_proj_kernel, tm=tm, aw=aw, bw=bw, cw=cw, mw=mw, n_cheads=nch,
                             taps_a=taps_a, taps_b=taps_b)
    return pl.pallas_call(
        kern, grid=(bsz, seq // tm), in_specs=in_specs, out_specs=out_specs, out_shape=out_shape,
        scratch_shapes=scratch,
        compiler_params=pltpu.CompilerParams(dimension_semantics=("arbitrary", "arbitrary"),
                                             vmem_limit_bytes=VMEM_LIMIT_BYTES),
        name="proj_branches",
    )(x, w["wa"], w["wb"], w["wkc"], w["wqvT"], w["wm"], w["bfn"], w["bfT"],
      w["caw"], w["cab"], w["lag"], w["lab"], w["cbw"], mkT, mv, tril, triu, selk, selq)


def _attn_kernel(qte_ref, ke_ref, vT_ref, sg_ref, o_ref, qh_ref, m_ref, acc_ref, sa_ref, sb_ref,
                 *, tq, tk, cw, n_heads):
    i = pl.program_id(1)
    diag = (i * tq) // tk
    tile = tq
    pair_w = 2 * HEAD_DIM
    ones_rows = jnp.ones((BF16_ROWS, tk), BF16)
    causal = (lax.broadcasted_iota(jnp.int32, (tk, tq), 0) + diag * tk
              <= lax.broadcasted_iota(jnp.int32, (tk, tq), 1) + i * tq)

    def build_query_operands(shift):
        extra = qte_ref[cw:, :].astype(F32)
        extra_row = lax.broadcasted_iota(jnp.int32, extra.shape, 0)
        feat_row = lax.broadcasted_iota(jnp.int32, (pair_w, tile), 0)
        for h in range(n_heads):
            pair, hh = divmod(h, 2)
            qpair = qte_ref[pair * pair_w:(pair + 1) * pair_w, :]
            own = (feat_row >= hh * HEAD_DIM) & (feat_row < (hh + 1) * HEAD_DIM)
            qm = jnp.where(own, qpair, jnp.zeros_like(qpair))
            k_rows, q_rows = 3 * h, 3 * (n_heads + h)
            parts = [extra[q_rows + r:q_rows + r + 1, :] for r in range(3)]
            if shift is not None:
                parts = [part.astype(F32) for part in _split3(parts[0] + parts[1] + parts[2] - shift[h])]
            em = jnp.where((extra_row >= k_rows) & (extra_row < k_rows + 3), 1.0, 0.0)
            for r in range(3):
                em = jnp.where(extra_row == q_rows + r, parts[r], em)
            qh_ref[h] = jnp.concatenate([qm, em.astype(BF16)], axis=0)

    def value_operand(h, start):
        return jnp.concatenate([vT_ref[h * HEAD_DIM:(h + 1) * HEAD_DIM, pl.ds(start, tk)], ones_rows], axis=0)

    def score_head(s_ref, j, h):
        start = pl.multiple_of(j * tk, tk)
        pair = h // 2
        kblk = jnp.concatenate([ke_ref[pl.ds(start, tk), pair * pair_w:(pair + 1) * pair_w],
                                ke_ref[pl.ds(start, tk), cw:]], axis=1)
        s_ref[h] = _dot(kblk, qh_ref[h])

    def scores_into(s_ref, j):
        for h in range(n_heads):
            score_head(s_ref, j, h)

    def absorb_head(s_ref, j, h, masked=False):
        start = pl.multiple_of(j * tk, tk)
        s = s_ref[h]
        if masked:
            s = jnp.where(causal, s, NEG_BIG)
        m_old = m_ref[h]
        m_new = jnp.maximum(m_old, jnp.max(s, axis=0, keepdims=True))
        p = jnp.exp2(s - m_new)
        alpha = jnp.exp2(m_old - m_new)
        m_ref[h] = m_new
        acc_ref[h] = alpha * acc_ref[h] + _dot(value_operand(h, start), p.astype(BF16))

    def absorb_referenced_head(s_ref, j, h):
        start = pl.multiple_of(j * tk, tk)
        acc_ref[h] = acc_ref[h] + _dot(value_operand(h, start), jnp.exp2(s_ref[h]).astype(BF16))

    def absorb(s_ref, j, absorb_fn, **kw):
        for h in range(n_heads):
            absorb_fn(s_ref, j, h, **kw)

    def pipelined(n_blocks, absorb_fn):
        def advance(nxt_ref, j_next, cur_ref, j_cur):
            for h in range(n_heads):
                score_head(nxt_ref, j_next, h)
                absorb_fn(cur_ref, j_cur, h)

        def two_blocks(jj, carry):
            j = 2 * jj
            advance(sb_ref, j + 1, sa_ref, j)
            advance(sa_ref, j + 2, sb_ref, j + 1)
            return carry

        @pl.when(n_blocks > 0)
        def _():
            scores_into(sa_ref, 0)

        lax.fori_loop(0, jnp.maximum(n_blocks - 1, 0) // 2, two_blocks, 0)

        @pl.when(n_blocks % 2 == 1)
        def _():
            absorb(sa_ref, n_blocks - 1, absorb_fn)

        @pl.when((n_blocks % 2 == 0) & (n_blocks > 0))
        def _():
            advance(sb_ref, n_blocks - 1, sa_ref, n_blocks - 2)
            absorb(sb_ref, n_blocks - 1, absorb_fn)

    build_query_operands(None)
    scores_into(sa_ref, diag)
    m0 = []
    diag_start = pl.multiple_of(diag * tk, tk)
    for h in range(n_heads):
        s = jnp.where(causal, sa_ref[h], NEG_BIG)
        m = jnp.max(s, axis=0, keepdims=True)
        acc_ref[h] = _dot(value_operand(h, diag_start), jnp.exp2(s - m).astype(BF16))
        m0.append(m)
    build_query_operands(m0)
    pipelined(diag, absorb_referenced_head)

    worst = jnp.max(jnp.abs(acc_ref[...].reshape(n_heads * (HEAD_DIM + BF16_ROWS), tile)))

    @pl.when(jnp.logical_not(worst < jnp.finfo(F32).max))
    def _():
        build_query_operands(None)
        m_ref[...] = jnp.full(m_ref.shape, NEG_BIG, F32)
        acc_ref[...] = jnp.zeros(acc_ref.shape, F32)
        pipelined(diag, absorb_head)
        scores_into(sa_ref, diag)
        absorb(sa_ref, diag, absorb_head, masked=True)

    for pair in range(n_heads // 2):
        oT = jnp.concatenate([acc_ref[h, :HEAD_DIM, :] / acc_ref[h, HEAD_DIM:HEAD_DIM + 1, :]
                              for h in (2 * pair, 2 * pair + 1)], axis=0)
        cols = slice(pair * pair_w, (pair + 1) * pair_w)
        o_ref[:, cols] = (oT.T * sg_ref[:, cols].astype(F32)).astype(BF16)


def _attn_call(qte, ke, vT, sg):
    bsz, cw, seq = vT.shape
    tile = min(ATTN_TILE, seq)
    tk = min(ATTN_KEY_TILE, seq)
    assert seq % tk == 0 and tk % tile == 0
    n_heads = cw // HEAD_DIM
    kw = cw + LANES
    return pl.pallas_call(
        functools.partial(_attn_kernel, tq=tile, tk=tk, cw=cw, n_heads=n_heads),
        grid=(bsz, seq // tile),
        in_specs=[
            pl.BlockSpec((None, kw, tile), lambda b, i: (b, 0, i)),
            pl.BlockSpec((None, seq, kw), lambda b, i: (b, 0, 0)),
            pl.BlockSpec((None, cw, seq), lambda b, i: (b, 0, 0)),
            pl.BlockSpec((None, tile, cw), lambda b, i: (b, i, 0)),
        ],
        out_specs=pl.BlockSpec((None, tile, cw), lambda b, i: (b, i, 0)),
        out_shape=jax.ShapeDtypeStruct((bsz, seq, cw), BF16),
        scratch_shapes=[
            pltpu.VMEM((n_heads, 2 * HEAD_DIM + LANES, tile), BF16),
            pltpu.VMEM((n_heads, 1, tile), F32),
            pltpu.VMEM((n_heads, HEAD_DIM + BF16_ROWS, tile), F32),
            pltpu.VMEM((n_heads, tk, tile), F32),
            pltpu.VMEM((n_heads, tk, tile), F32),
        ],
        compiler_params=pltpu.CompilerParams(dimension_semantics=("arbitrary", "arbitrary"),
                                             vmem_limit_bytes=VMEM_LIMIT_BYTES),
        name="forget_attn",
    )(qte, ke, vT, sg)


def _merge_kernel(x_ref, ha_ref, hb_ref, hc_ref, hm_ref, wg_ref, pa_ref, pb_ref, pc_ref, pm_ref,
                  wo_ref, g_ref, b_ref, o_ref, *, alpha):
    x = x_ref[...]
    xb = x.astype(BF16)
    d = x.shape[-1]
    merged = None
    for n, (h_ref, p_ref) in enumerate(((ha_ref, pa_ref), (hb_ref, pb_ref), (hc_ref, pc_ref), (hm_ref, pm_ref))):
        gate = _sigmoid(_dot(xb, wg_ref[:, n * d:(n + 1) * d]))
        term = gate * _dot(h_ref[...], p_ref[...])
        merged = term if merged is None else merged + term
    out = _dot(merged.astype(BF16), wo_ref[...])
    o_ref[...] = _layer_norm(alpha * x + out, g_ref[...], b_ref[...])


def _merge_call(x, ha, hb, hc, hm, layer, w, alpha):
    bsz, seq, d = x.shape
    tm = min(TOKEN_TILE, seq)
    row = lambda width: pl.BlockSpec((None, tm, width), lambda i, s: (i, s, 0))
    in_specs = [row(d), row(ha.shape[-1]), row(hb.shape[-1]), row(hc.shape[-1]), row(hm.shape[-1])]
    names = ("wg", "pa", "pb", "pc", "pm", "wo", "lng", "lnb")
    in_specs += [_const_spec(w[n].shape[1:], layer) for n in names]
    return pl.pallas_call(
        functools.partial(_merge_kernel, alpha=alpha),
        grid=(bsz, seq // tm), in_specs=in_specs, out_specs=row(d),
        out_shape=jax.ShapeDtypeStruct((bsz, seq, d), F32),
        compiler_params=pltpu.CompilerParams(dimension_semantics=("arbitrary", "arbitrary"),
                                             vmem_limit_bytes=VMEM_LIMIT_BYTES),
        name="merge_out_norm",
    )(x, ha, hb, hc, hm, *[w[n] for n in names])


def _prepare_weights(w_in, b_forget, conv_a_w, conv_a_b, ln_a_g, ln_a_b, conv_b_w, p_a, p_b, p_c, p_m,
                     w_out, ln_g, ln_b, dims):
    aw, bw, cw, mw, nch, d = dims["aw"], dims["bw"], dims["cw"], dims["mw"], dims["n_cheads"], dims["d"]
    depth = w_in.shape[0]
    cuts = [0]
    for width in (3 * aw, 4 * bw, cw, cw, cw, nch, cw, 2 * mw, 4 * d):
        cuts.append(cuts[-1] + width)
    assert cuts[-1] == w_in.shape[-1]
    seg = lambda n: w_in[:, :, cuts[n]:cuts[n + 1]]
    wq, wk, wv, wf, wcg = seg(2), seg(3), seg(4), seg(5), seg(6)
    wf_cols = jnp.pad(wf, ((0, 0), (0, 0), (0, LANES - nch)))
    wf_rows = jnp.pad(jnp.swapaxes(wf, 1, 2), ((0, 0), (0, BF16_ROWS - nch), (0, 0)))
    bf_pad_n = jnp.pad(b_forget, ((0, 0), (0, LANES - nch)))[:, None, :]
    bf_pad_t = jnp.broadcast_to(jnp.pad(b_forget, ((0, 0), (0, BF16_ROWS - nch)))[:, :, None],
                                (depth, BF16_ROWS, LANES))
    return {
        "wa": seg(0).astype(BF16), "wb": seg(1).astype(BF16),
        "wkc": jnp.concatenate([wk, wcg, wf_cols], axis=-1).astype(BF16),
        "wqvT": jnp.concatenate([jnp.swapaxes(wq, 1, 2), jnp.swapaxes(wv, 1, 2), wf_rows], axis=1).astype(BF16),
        "wm": seg(7).astype(BF16), "wg": seg(8).astype(BF16),
        "bfn": bf_pad_n, "bfT": bf_pad_t,
        "caw": conv_a_w, "cab": conv_a_b[:, None, :], "lag": ln_a_g[:, None, :], "lab": ln_a_b[:, None, :],
        "cbw": conv_b_w,
        "pa": p_a.astype(BF16), "pb": p_b.astype(BF16), "pc": p_c.astype(BF16), "pm": p_m.astype(BF16),
        "wo": w_out.astype(BF16), "lng": ln_g[:, None, :], "lnb": ln_b[:, None, :],
    }


def _decay_selectors(n_heads):
    assert 6 * n_heads <= LANES and n_heads <= BF16_ROWS
    selk = np.zeros((3 * LANES, LANES), np.float32)
    selq = np.zeros((LANES, 3 * BF16_ROWS), np.float32)
    for h in range(n_heads):
        for r in range(3):
            selk[r * LANES + h, 3 * h + r] = -1.0
            selq[3 * (n_heads + h) + r, r * BF16_ROWS + h] = 1.0
    return jnp.asarray(selk, BF16), jnp.asarray(selq, BF16)


def kernel(x, mem, w_in, b_forget, conv_a_w, conv_a_b, ln_a_g, ln_a_b, conv_b_w, w_kv_mem, mem_ln_g, mem_ln_b,
           p_a, p_b, p_c, p_m, w_out, ln_g, ln_b):
    bsz, seq, d = x.shape
    depth = w_in.shape[0]
    dims = {"d": d, "aw": conv_a_w.shape[-1], "bw": conv_b_w.shape[-1], "cw": p_c.shape[1], "mw": p_m.shape[1],
            "n_cheads": b_forget.shape[-1]}
    assert dims["cw"] == dims["n_cheads"] * HEAD_DIM and dims["n_cheads"] % 2 == 0
    assert dims["mw"] % HEAD_DIM == 0 and seq % min(TOKEN_TILE, seq) == 0 and seq % min(ATTN_TILE, seq) == 0
    assert conv_a_w.shape[1] - 1 <= A_HIST and conv_b_w.shape[1] - 1 <= B_HIST
    alpha = (2.0 * depth) ** 0.25
    w = _prepare_weights(w_in, b_forget, conv_a_w, conv_a_b, ln_a_g, ln_a_b, conv_b_w, p_a, p_b, p_c, p_m,
                         w_out, ln_g, ln_b, dims)
    mw = dims["mw"]
    wkT_mem = jnp.swapaxes(w_kv_mem[:, :, :mw], 1, 2).astype(BF16)
    wv_mem = w_kv_mem[:, :, mw:].astype(BF16)
    mkT, mv = _mem_call(mem, mem_ln_g[None, :], mem_ln_b[None, :], wkT_mem, wv_mem)
    tm = min(TOKEN_TILE, seq)
    tril = jnp.tril(jnp.ones((tm, tm), BF16))
    triu = jnp.triu(jnp.ones((tm, tm), BF16))
    selk, selq = _decay_selectors(dims["n_cheads"])
    for layer in range(depth):
        ha, hb, ke, sgc, qte, vT, hm = _proj_call(x, layer, w, mkT, mv, tril, triu, selk, selq, dims)
        hc = _attn_call(qte, ke, vT, sgc)
        x = _merge_call(x, ha, hb, hc, hm, layer, w, alpha)
    return x
```

```python
import functools

import jax
import jax.numpy as jnp
import numpy as np
from jax import lax
from jax.experimental import pallas as pl
from jax.experimental.pallas import tpu as pltpu

F32 = jnp.float32
BF16 = jnp.bfloat16

LN_EPS = 1e-5
HEAD_DIM = 64
LANES = 128
SUBLANES = 8
BF16_ROWS = 16
VMEM_LIMIT_BYTES = 56 * 1024 * 1024
NEG_BIG = -1e30
LOG2E = 1.4426950408889634

TOKEN_TILE = 512
ATTN_TILE = 512
ATTN_KEY_TILE = 256
CONV_ROW_CHUNK = 32
A_HIST = 32
B_HIST = 8


def _dot(a, b):
    return jnp.dot(a, b, preferred_element_type=F32)


def _dot_nt(a, b):
    return lax.dot_general(a, b, (((1,), (1,)), ((), ())), preferred_element_type=F32)


def _sigmoid(v):
    return 1.0 / (1.0 + jnp.exp(-v))


def _silu(v):
    return v * _sigmoid(v)


def _log_sigmoid(v):
    return jnp.minimum(v, 0.0) - jnp.log1p(jnp.exp(-jnp.abs(v)))


def _layer_norm(v, g, b):
    mu = jnp.mean(v, axis=-1, keepdims=True)
    c = v - mu
    var = jnp.mean(c * c, axis=-1, keepdims=True)
    return c * lax.rsqrt(var + LN_EPS) * g + b


def _split3(v):
    hi = v.astype(BF16)
    r1 = v - hi.astype(F32)
    mid = r1.astype(BF16)
    lo = (r1 - mid.astype(F32)).astype(BF16)
    return hi, mid, lo


def _const_spec(shape, layer):
    nd = len(shape)
    return pl.BlockSpec((None,) + tuple(shape), lambda *_, l=layer, nd=nd: (l,) + (0,) * nd,
                        pipeline_mode=pl.Buffered(1))


def _mem_kernel(mem_ref, g_ref, b_ref, wkT_ref, wv_ref, mkT_ref, mv_ref, *, n_heads):
    mem_n = _layer_norm(mem_ref[...], g_ref[...], b_ref[...]).astype(BF16)
    mkT = _dot_nt(wkT_ref[...], mem_n)
    mv = _dot(mem_n, wv_ref[...])
    row_head = lax.broadcasted_iota(jnp.int32, mkT.shape, 0) // HEAD_DIM
    col_head = lax.broadcasted_iota(jnp.int32, mv.shape, 1) // HEAD_DIM
    for h in range(n_heads):
        mkT_ref[h] = jnp.where(row_head == h, mkT, 0.0).astype(BF16)
        mv_ref[h] = jnp.where(col_head == h, mv, 0.0).astype(BF16)


def _mem_call(mem, g, b, wkT, wv):
    bsz, mlen, d = mem.shape
    depth, mw, _ = wkT.shape
    n_heads = mw // HEAD_DIM
    out = jax.ShapeDtypeStruct((bsz, depth, n_heads, mw, mlen), BF16), \
        jax.ShapeDtypeStruct((bsz, depth, n_heads, mlen, mw), BF16)
    return pl.pallas_call(
        functools.partial(_mem_kernel, n_heads=n_heads),
        grid=(bsz, depth),
        in_specs=[
            pl.BlockSpec((None, mlen, d), lambda i, l: (i, 0, 0)),
            pl.BlockSpec((1, d), lambda i, l: (0, 0)),
            pl.BlockSpec((1, d), lambda i, l: (0, 0)),
            pl.BlockSpec((None, mw, d), lambda i, l: (l, 0, 0)),
            pl.BlockSpec((None, d, mw), lambda i, l: (l, 0, 0)),
        ],
        out_specs=[
            pl.BlockSpec((None, None, n_heads, mw, mlen), lambda i, l: (i, l, 0, 0, 0)),
            pl.BlockSpec((None, None, n_heads, mlen, mw), lambda i, l: (i, l, 0, 0, 0)),
        ],
        out_shape=out,
        compiler_params=pltpu.CompilerParams(dimension_semantics=("arbitrary", "arbitrary"),
                                             vmem_limit_bytes=VMEM_LIMIT_BYTES),
        name="mem_kv",
    )(mem, g, b, wkT, wv)


def _proj_kernel(x_ref, wa_ref, wb_ref, wkc_ref, wqvT_ref, wm_ref, bfn_ref, bfT_ref,
                 caw_ref, cab_ref, lag_ref, lab_ref, cbw_ref, mkT_ref, mv_ref, tril_ref, triu_ref,
                 selk_ref, selq_ref,
                 ha_ref, hb_ref, ke_ref, sgc_ref, qte_ref, vT_ref, hm_ref,
                 abuf, ashift, aconv, bbuf, carry_n, carry_t,
                 *, tm, aw, bw, cw, mw, n_cheads, taps_a, taps_b):
    @pl.when(pl.program_id(1) == 0)
    def _():
        abuf[0:A_HIST, :] = jnp.zeros((A_HIST, aw), F32)
        bbuf[0:B_HIST, :] = jnp.zeros((B_HIST, bw), F32)
        carry_n[...] = jnp.zeros_like(carry_n)
        carry_t[...] = jnp.zeros_like(carry_t)

    xb = x_ref[...].astype(BF16)

    xm = _dot(xb, wm_ref[...])
    xa = _dot(xb, wa_ref[...])
    mq = (xm[:, :mw] * (HEAD_DIM ** -0.5)).astype(BF16)
    mem_scores = [_dot(mq, mkT_ref[h]) for h in range(mw // HEAD_DIM)]
    xbm = _dot(xb, wb_ref[...])
    xk = _dot(xb, wkc_ref[...])
    qv = _dot_nt(wqvT_ref[...], xb)

    abuf[A_HIST:A_HIST + tm, :] = xa[:, :aw] * _sigmoid(xa[:, aw:2 * aw])
    n_shift_rows = tm + A_HIST - SUBLANES
    for sh in range(1, SUBLANES):
        ashift[sh - 1] = abuf[sh:sh + n_shift_rows, :]
    first = A_HIST - (taps_a - 1)
    for r0 in range(0, tm, CONV_ROW_CHUNK):
        acc = jnp.broadcast_to(cab_ref[...], (CONV_ROW_CHUNK, aw))
        for t in range(taps_a):
            blk, sh = divmod(first + t, SUBLANES)
            start = r0 + blk * SUBLANES
            if sh == 0:
                rows = abuf[start:start + CONV_ROW_CHUNK, :]
            else:
                rows = ashift[sh - 1, start:start + CONV_ROW_CHUNK, :]
            acc = acc + jnp.tile(caw_ref[t], (CONV_ROW_CHUNK // SUBLANES, 1)) * rows
        aconv[r0:r0 + CONV_ROW_CHUNK, :] = acc
    abuf[0:A_HIST, :] = abuf[tm:tm + A_HIST, :]
    a = _silu(_layer_norm(aconv[...], lag_ref[...], lab_ref[...]))
    ha_ref[...] = (a * _silu(xa[:, 2 * aw:])).astype(BF16)

    om = jnp.zeros((tm, mw), F32)
    for h, s in enumerate(mem_scores):
        e = jnp.exp(s - jnp.max(s, axis=-1, keepdims=True))
        p = e / jnp.sum(e, axis=-1, keepdims=True)
        om = om + _dot(p.astype(BF16), mv_ref[h])
    hm_ref[...] = (om * _silu(xm[:, mw:])).astype(BF16)

    bbuf[B_HIST:B_HIST + tm, :] = xbm[:, 2 * bw:3 * bw] * xbm[:, :bw]
    firstb = B_HIST - (taps_b - 1)
    convb = cbw_ref[0:1, :] * bbuf[firstb:firstb + tm, :]
    for t in range(1, taps_b):
        convb = convb + cbw_ref[t:t + 1, :] * bbuf[firstb + t:firstb + t + tm, :]
    bbuf[0:B_HIST, :] = bbuf[tm:tm + B_HIST, :]
    hb_ref[...] = (xbm[:, bw:2 * bw] * convb * _silu(xbm[:, 3 * bw:])).astype(BF16)

    sgc_ref[...] = _silu(xk[:, cw:2 * cw]).astype(BF16)
    lf = _log_sigmoid(xk[:, 2 * cw:] + bfn_ref[...])
    hi, mid, lo = _split3(lf)
    tril = tril_ref[...]
    cum = _dot(tril, hi) + _dot(tril, mid) + _dot(tril, lo) + carry_n[...]
    carry_n[...] = cum[tm - 1:tm, :]
    parts = jnp.concatenate(_split3(cum * LOG2E), axis=1)
    lane = lax.broadcasted_iota(jnp.int32, (tm, LANES), 1)
    in_slab = lane & (BF16_ROWS - 1)
    ones_cols = jnp.where((in_slab >= 3) & (in_slab < 6) & (lane < BF16_ROWS * n_cheads), 1.0, 0.0)
    ke_ref[:, :cw] = xk[:, :cw].astype(BF16)
    ke_ref[:, cw:] = (_dot(parts, selk_ref[...]) + ones_cols).astype(BF16)

    vT_ref[...] = qv[cw:2 * cw].astype(BF16)
    lft = _log_sigmoid(qv[2 * cw:] + bfT_ref[...][:, 0:1])
    hi, mid, lo = _split3(lft)
    triu = triu_ref[...]
    cumt = _dot(hi, triu) + _dot(mid, triu) + _dot(lo, triu) + carry_t[...][:, 0:1]
    carry_t[...] = jnp.broadcast_to(cumt[:, tm - 1:tm], carry_t.shape)
    parts_t = jnp.concatenate(_split3(cumt * LOG2E), axis=0)
    sub = lax.broadcasted_iota(jnp.int32, (LANES, tm), 0)
    ones_rows = jnp.where(((sub & (BF16_ROWS - 1)) < 3) & (sub < BF16_ROWS * n_cheads), 1.0, 0.0)
    qte_ref[:cw, :] = (qv[:cw] * (LOG2E * HEAD_DIM ** -0.5)).astype(BF16)
    qte_ref[cw:, :] = (_dot(selq_ref[...], parts_t) + ones_rows).astype(BF16)


def _proj_call(x, layer, w, mkT, mv, tril, triu, selk, selq, dims):
    bsz, seq, d = x.shape
    tm = min(TOKEN_TILE, seq)
    aw, bw, cw, mw, nch = dims["aw"], dims["bw"], dims["cw"], dims["mw"], dims["n_cheads"]
    taps_a, taps_b = w["caw"].shape[1], w["cbw"].shape[1]
    n_heads_m = mw // HEAD_DIM
    mlen = mv.shape[3]
    row = lambda width: pl.BlockSpec((None, tm, width), lambda i, s: (i, s, 0))
    col = lambda height: pl.BlockSpec((None, height, tm), lambda i, s: (i, 0, s))
    in_specs = [
        row(d),
        _const_spec(w["wa"].shape[1:], layer), _const_spec(w["wb"].shape[1:], layer),
        _const_spec(w["wkc"].shape[1:], layer), _const_spec(w["wqvT"].shape[1:], layer),
        _const_spec(w["wm"].shape[1:], layer),
        _const_spec(w["bfn"].shape[1:], layer), _const_spec(w["bfT"].shape[1:], layer),
        _const_spec(w["caw"].shape[1:], layer), _const_spec(w["cab"].shape[1:], layer),
        _const_spec(w["lag"].shape[1:], layer), _const_spec(w["lab"].shape[1:], layer),
        _const_spec(w["cbw"].shape[1:], layer),
        pl.BlockSpec((None, None, n_heads_m, mw, mlen), lambda i, s, l=layer: (i, l, 0, 0, 0)),
        pl.BlockSpec((None, None, n_heads_m, mlen, mw), lambda i, s, l=layer: (i, l, 0, 0, 0)),
        pl.BlockSpec((tm, tm), lambda i, s: (0, 0), pipeline_mode=pl.Buffered(1)),
        pl.BlockSpec((tm, tm), lambda i, s: (0, 0), pipeline_mode=pl.Buffered(1)),
        pl.BlockSpec(selk.shape, lambda i, s: (0, 0), pipeline_mode=pl.Buffered(1)),
        pl.BlockSpec(selq.shape, lambda i, s: (0, 0), pipeline_mode=pl.Buffered(1)),
    ]
    out_specs = [row(aw), row(bw), row(cw + LANES), row(cw), col(cw + LANES), col(cw), row(mw)]
    out_shape = [
        jax.ShapeDtypeStruct((bsz, seq, aw), BF16), jax.ShapeDtypeStruct((bsz, seq, bw), BF16),
        jax.ShapeDtypeStruct((bsz, seq, cw + LANES), BF16), jax.ShapeDtypeStruct((bsz, seq, cw), BF16),
        jax.ShapeDtypeStruct((bsz, cw + LANES, seq), BF16), jax.ShapeDtypeStruct((bsz, cw, seq), BF16),
        jax.ShapeDtypeStruct((bsz, seq, mw), BF16),
    ]
    scratch = [
        pltpu.VMEM((A_HIST + tm, aw), F32),
        pltpu.VMEM((SUBLANES - 1, tm + A_HIST - SUBLANES, aw), F32),
        pltpu.VMEM((tm, aw), F32),
        pltpu.VMEM((B_HIST + tm, bw), F32),
        pltpu.VMEM((1, LANES), F32),
        pltpu.VMEM((BF16_ROWS, LANES), F32),
    ]
    kern = functools.partial(_proj_kernel, tm=tm, aw=aw, bw=bw, cw=cw, mw=mw, n_cheads=nch,
                             taps_a=taps_a, taps_b=taps_b)
    return pl.pallas_call(
        kern, grid=(bsz, seq // tm), in_specs=in_specs, out_specs=out_specs, out_shape=out_shape,
        scratch_shapes=scratch,
        compiler_params=pltpu.CompilerParams(dimension_semantics=("arbitrary", "arbitrary"),
                                             vmem_limit_bytes=VMEM_LIMIT_BYTES),
        name="proj_branches",
    )(x, w["wa"], w["wb"], w["wkc"], w["wqvT"], w["wm"], w["bfn"], w["bfT"],
      w["caw"], w["cab"], w["lag"], w["lab"], w["cbw"], mkT, mv, tril, triu, selk, selq)


def _attn_kernel(qte_ref, ke_ref, vT_ref, sg_ref, o_ref, qh_ref, m_ref, acc_ref, sa_ref, sb_ref,
                 *, tq, tk, cw, n_heads):
    i = pl.program_id(1)
    n_diag = tq // tk
    n_before = i * n_diag
    tile = tq
    pair_w = 2 * HEAD_DIM
    ones_rows = jnp.ones((BF16_ROWS, tk), BF16)
    key_pos = lax.broadcasted_iota(jnp.int32, (tk, tq), 0)
    query_pos = lax.broadcasted_iota(jnp.int32, (tk, tq), 1)
    causal = [key_pos + d * tk <= query_pos for d in range(n_diag)]

    def build_query_operands():
        feat_row = lax.broadcasted_iota(jnp.int32, (pair_w, tile), 0)
        for h in range(n_heads):
            pair, hh = divmod(h, 2)
            qpair = qte_ref[pair * pair_w:(pair + 1) * pair_w, :]
            own = (feat_row >= hh * HEAD_DIM) & (feat_row < (hh + 1) * HEAD_DIM)
            qh_ref[h, :pair_w, :] = jnp.where(own, qpair, jnp.zeros_like(qpair))
            qh_ref[h, pair_w:, :] = jnp.zeros((LANES, tile), BF16)
            slab = slice(BF16_ROWS * h, BF16_ROWS * (h + 1))
            qh_ref[h, pair_w + slab.start:pair_w + slab.stop, :] = qte_ref[cw + slab.start:cw + slab.stop, :]

    def set_reference(shift):
        slab_row = lax.broadcasted_iota(jnp.int32, (BF16_ROWS, tile), 0)
        for h in range(n_heads):
            lo_row = cw + BF16_ROWS * h
            slab = qte_ref[lo_row:lo_row + BF16_ROWS, :].astype(F32)
            parts = _split3(slab[3:4, :] + slab[4:5, :] + slab[5:6, :] - shift[h])
            new = jnp.where(slab_row < 3, 1.0, 0.0)
            for r in range(3):
                new = jnp.where(slab_row == 3 + r, parts[r].astype(F32), new)
            qh_ref[h, pair_w + BF16_ROWS * h:pair_w + BF16_ROWS * (h + 1), :] = new.astype(BF16)

    def value_operand(h, start):
        return jnp.concatenate([vT_ref[h * HEAD_DIM:(h + 1) * HEAD_DIM, pl.ds(start, tk)], ones_rows], axis=0)

    def score_head(s_ref, j, h):
        start = pl.multiple_of(j * tk, tk)
        pair = h // 2
        kblk = jnp.concatenate([ke_ref[pl.ds(start, tk), pair * pair_w:(pair + 1) * pair_w],
                                ke_ref[pl.ds(start, tk), cw:]], axis=1)
        s_ref[h] = _dot(kblk, qh_ref[h])

    def scores_into(s_ref, j):
        for h in range(n_heads):
            score_head(s_ref, j, h)

    def absorb_head(s_ref, j, h, mask=None):
        start = pl.multiple_of(j * tk, tk)
        s = s_ref[h]
        if mask is not None:
            s = jnp.where(mask, s, NEG_BIG)
        m_old = m_ref[h]
        m_new = jnp.maximum(m_old, jnp.max(s, axis=0, keepdims=True))
        p = jnp.exp2(s - m_new)
        alpha = jnp.exp2(m_old - m_new)
        m_ref[h] = m_new
        acc_ref[h] = alpha * acc_ref[h] + _dot(value_operand(h, start), p.astype(BF16))

    def absorb_referenced_head(s_ref, j, h):
        start = pl.multiple_of(j * tk, tk)
        acc_ref[h] = acc_ref[h] + _dot(value_operand(h, start), jnp.exp2(s_ref[h]).astype(BF16))

    def absorb(s_ref, j, absorb_fn, **kw):
        for h in range(n_heads):
            absorb_fn(s_ref, j, h, **kw)

    def pipelined(n_blocks, absorb_fn):
        def advance(nxt_ref, j_next, cur_ref, j_cur):
            for h in range(n_heads):
                score_head(nxt_ref, j_next, h)
                absorb_fn(cur_ref, j_cur, h)

        def two_blocks(jj, carry):
            j = 2 * jj
            advance(sb_ref, j + 1, sa_ref, j)
            advance(sa_ref, j + 2, sb_ref, j + 1)
            return carry

        @pl.when(n_blocks > 0)
        def _():
            scores_into(sa_ref, 0)

        lax.fori_loop(0, jnp.maximum(n_blocks - 1, 0) // 2, two_blocks, 0)

        @pl.when(n_blocks % 2 == 1)
        def _():
            absorb(sa_ref, n_blocks - 1, absorb_fn)

        @pl.when((n_blocks % 2 == 0) & (n_blocks > 0))
        def _():
            advance(sb_ref, n_blocks - 1, sa_ref, n_blocks - 2)
            absorb(sb_ref, n_blocks - 1, absorb_fn)

    assert n_diag <= 2
    diag_refs = (sa_ref, sb_ref)[:n_diag]
    build_query_operands()
    for d, s_ref in enumerate(diag_refs):
        scores_into(s_ref, n_before + d)
    m0 = []
    for h in range(n_heads):
        tiles = [jnp.where(causal[d], s_ref[h], NEG_BIG) for d, s_ref in enumerate(diag_refs)]
        m = jnp.max(tiles[0], axis=0, keepdims=True)
        for s in tiles[1:]:
            m = jnp.maximum(m, jnp.max(s, axis=0, keepdims=True))
        total = None
        for d, s in enumerate(tiles):
            start = pl.multiple_of((n_before + d) * tk, tk)
            part = _dot(value_operand(h, start), jnp.exp2(s - m).astype(BF16))
            total = part if total is None else total + part
        acc_ref[h] = total
        m0.append(m)
    set_reference(m0)
    pipelined(n_before, absorb_referenced_head)

    worst = jnp.max(jnp.abs(acc_ref[...].reshape(n_heads * (HEAD_DIM + BF16_ROWS), tile)))

    @pl.when(jnp.logical_not(worst < jnp.finfo(F32).max))
    def _():
        build_query_operands()
        m_ref[...] = jnp.full(m_ref.shape, NEG_BIG, F32)
        acc_ref[...] = jnp.zeros(acc_ref.shape, F32)
        pipelined(n_before, absorb_head)
        for d, s_ref in enumerate(diag_refs):
            scores_into(s_ref, n_before + d)
        for d, s_ref in enumerate(diag_refs):
            absorb(s_ref, n_before + d, absorb_head, mask=causal[d])

    for pair in range(n_heads // 2):
        oT = jnp.concatenate([acc_ref[h, :HEAD_DIM, :] / acc_ref[h, HEAD_DIM:HEAD_DIM + 1, :]
                              for h in (2 * pair, 2 * pair + 1)], axis=0)
        cols = slice(pair * pair_w, (pair + 1) * pair_w)
        o_ref[:, cols] = (oT.T * sg_ref[:, cols].astype(F32)).astype(BF16)


def _attn_call(qte, ke, vT, sg):
    bsz, cw, seq = vT.shape
    tile = min(ATTN_TILE, seq)
    tk = min(ATTN_KEY_TILE, seq)
    assert seq % tile == 0 and tile % tk == 0
    n_heads = cw // HEAD_DIM
    kw = cw + LANES
    return pl.pallas_call(
        functools.partial(_attn_kernel, tq=tile, tk=tk, cw=cw, n_heads=n_heads),
        grid=(bsz, seq // tile),
        in_specs=[
            pl.BlockSpec((None, kw, tile), lambda b, i: (b, 0, i)),
            pl.BlockSpec((None, seq, kw), lambda b, i: (b, 0, 0), pipeline_mode=pl.Buffered(1)),
            pl.BlockSpec((None, cw, seq), lambda b, i: (b, 0, 0), pipeline_mode=pl.Buffered(1)),
            pl.BlockSpec((None, tile, cw), lambda b, i: (b, i, 0)),
        ],
        out_specs=pl.BlockSpec((None, tile, cw), lambda b, i: (b, i, 0)),
        out_shape=jax.ShapeDtypeStruct((bsz, seq, cw), BF16),
        scratch_shapes=[
            pltpu.VMEM((n_heads, 2 * HEAD_DIM + LANES, tile), BF16),
            pltpu.VMEM((n_heads, 1, tile), F32),
            pltpu.VMEM((n_heads, HEAD_DIM + BF16_ROWS, tile), F32),
            pltpu.VMEM((n_heads, tk, tile), F32),
            pltpu.VMEM((n_heads, tk, tile), F32),
        ],
        compiler_params=pltpu.CompilerParams(dimension_semantics=("arbitrary", "arbitrary"),
                                             vmem_limit_bytes=VMEM_LIMIT_BYTES),
        name="forget_attn",
    )(qte, ke, vT, sg)


def _merge_kernel(x_ref, ha_ref, hb_ref, hc_ref, hm_ref, wg_ref, pa_ref, pb_ref, pc_ref, pm_ref,
                  wo_ref, g_ref, b_ref, o_ref, *, alpha):
    x = x_ref[...]
    xb = x.astype(BF16)
    d = x.shape[-1]
    merged = None
    for n, (h_ref, p_ref) in enumerate(((ha_ref, pa_ref), (hb_ref, pb_ref), (hc_ref, pc_ref), (hm_ref, pm_ref))):
        gate = _sigmoid(_dot(xb, wg_ref[:, n * d:(n + 1) * d]))
        term = gate * _dot(h_ref[...], p_ref[...])
        merged = term if merged is None else merged + term
    out = _dot(merged.astype(BF16), wo_ref[...])
    o_ref[...] = _layer_norm(alpha * x + out, g_ref[...], b_ref[...])


def _merge_call(x, ha, hb, hc, hm, layer, w, alpha):
    bsz, seq, d = x.shape
    tm = min(TOKEN_TILE, seq)
    row = lambda width: pl.BlockSpec((None, tm, width), lambda i, s: (i, s, 0))
    in_specs = [row(d), row(ha.shape[-1]), row(hb.shape[-1]), row(hc.shape[-1]), row(hm.shape[-1])]
    names = ("wg", "pa", "pb", "pc", "pm", "wo", "lng", "lnb")
    in_specs += [_const_spec(w[n].shape[1:], layer) for n in names]
    return pl.pallas_call(
        functools.partial(_merge_kernel, alpha=alpha),
        grid=(bsz, seq // tm), in_specs=in_specs, out_specs=row(d),
        out_shape=jax.ShapeDtypeStruct((bsz, seq, d), F32),
        compiler_params=pltpu.CompilerParams(dimension_semantics=("arbitrary", "arbitrary"),
                                             vmem_limit_bytes=VMEM_LIMIT_BYTES),
        name="merge_out_norm",
    )(x, ha, hb, hc, hm, *[w[n] for n in names])


def _prepare_weights(w_in, b_forget, conv_a_w, conv_a_b, ln_a_g, ln_a_b, conv_b_w, p_a, p_b, p_c, p_m,
                     w_out, ln_g, ln_b, dims):
    aw, bw, cw, mw, nch, d = dims["aw"], dims["bw"], dims["cw"], dims["mw"], dims["n_cheads"], dims["d"]
    depth = w_in.shape[0]
    cuts = [0]
    for width in (3 * aw, 4 * bw, cw, cw, cw, nch, cw, 2 * mw, 4 * d):
        cuts.append(cuts[-1] + width)
    assert cuts[-1] == w_in.shape[-1]
    seg = lambda n: w_in[:, :, cuts[n]:cuts[n + 1]]
    wq, wk, wv, wf, wcg = seg(2), seg(3), seg(4), seg(5), seg(6)
    wf_cols = jnp.pad(wf, ((0, 0), (0, 0), (0, LANES - nch)))
    wf_rows = jnp.pad(jnp.swapaxes(wf, 1, 2), ((0, 0), (0, BF16_ROWS - nch), (0, 0)))
    bf_pad_n = jnp.pad(b_forget, ((0, 0), (0, LANES - nch)))[:, None, :]
    bf_pad_t = jnp.broadcast_to(jnp.pad(b_forget, ((0, 0), (0, BF16_ROWS - nch)))[:, :, None],
                                (depth, BF16_ROWS, LANES))
    return {
        "wa": seg(0).astype(BF16), "wb": seg(1).astype(BF16),
        "wkc": jnp.concatenate([wk, wcg, wf_cols], axis=-1).astype(BF16),
        "wqvT": jnp.concatenate([jnp.swapaxes(wq, 1, 2), jnp.swapaxes(wv, 1, 2), wf_rows], axis=1).astype(BF16),
        "wm": seg(7).astype(BF16), "wg": seg(8).astype(BF16),
        "bfn": bf_pad_n, "bfT": bf_pad_t,
        "caw": jnp.broadcast_to(conv_a_w[:, :, None, :], conv_a_w.shape[:2] + (SUBLANES, aw)),
        "cab": conv_a_b[:, None, :], "lag": ln_a_g[:, None, :], "lab": ln_a_b[:, None, :],
        "cbw": conv_b_w,
        "pa": p_a.astype(BF16), "pb": p_b.astype(BF16), "pc": p_c.astype(BF16), "pm": p_m.astype(BF16),
        "wo": w_out.astype(BF16), "lng": ln_g[:, None, :], "lnb": ln_b[:, None, :],
    }


def _decay_selectors(n_heads):
    assert BF16_ROWS * n_heads <= LANES and n_heads <= BF16_ROWS
    selk = np.zeros((3 * LANES, LANES), np.float32)
    selq = np.zeros((LANES, 3 * BF16_ROWS), np.float32)
    for h in range(n_heads):
        for r in range(3):
            selk[r * LANES + h, BF16_ROWS * h + r] = -1.0
            selq[BF16_ROWS * h + 3 + r, r * BF16_ROWS + h] = 1.0
    return jnp.asarray(selk, BF16), jnp.asarray(selq, BF16)


def kernel(x, mem, w_in, b_forget, conv_a_w, conv_a_b, ln_a_g, ln_a_b, conv_b_w, w_kv_mem, mem_ln_g, mem_ln_b,
           p_a, p_b, p_c, p_m, w_out, ln_g, ln_b):
    bsz, seq, d = x.shape
    depth = w_in.shape[0]
    dims = {"d": d, "aw": conv_a_w.shape[-1], "bw": conv_b_w.shape[-1], "cw": p_c.shape[1], "mw": p_m.shape[1],
            "n_cheads": b_forget.shape[-1]}
    assert dims["cw"] == dims["n_cheads"] * HEAD_DIM and dims["n_cheads"] % 2 == 0
    assert dims["mw"] % HEAD_DIM == 0 and seq % min(TOKEN_TILE, seq) == 0 and seq % min(ATTN_TILE, seq) == 0
    assert conv_a_w.shape[1] - 1 <= A_HIST and conv_b_w.shape[1] - 1 <= B_HIST
    alpha = (2.0 * depth) ** 0.25
    w = _prepare_weights(w_in, b_forget, conv_a_w, conv_a_b, ln_a_g, ln_a_b, conv_b_w, p_a, p_b, p_c, p_m,
                         w_out, ln_g, ln_b, dims)
    mw = dims["mw"]
    wkT_mem = jnp.swapaxes(w_kv_mem[:, :, :mw], 1, 2).astype(BF16)
    wv_mem = w_kv_mem[:, :, mw:].astype(BF16)
    mkT, mv = _mem_call(mem, mem_ln_g[None, :], mem_ln_b[None, :], wkT_mem, wv_mem)
    tm = min(TOKEN_TILE, seq)
    tril = jnp.tril(jnp.ones((tm, tm), BF16))
    triu = jnp.triu(jnp.ones((tm, tm), BF16))
    selk, selq = _decay_selectors(dims["n_cheads"])
    for layer in range(depth):
        ha, hb, ke, sgc, qte, vT, hm = _proj_call(x, layer, w, mkT, mv, tril, triu, selk, selq, dims)
        hc = _attn_call(qte, ke, vT, sgc)
        x = _merge_call(x, ha, hb, hc, hm, layer, w, alpha)
    return x
```

```python
import functools

import jax
import jax.numpy as jnp
import numpy as np
from jax import lax
from jax.experimental import pallas as pl
from jax.experimental.pallas import tpu as pltpu

F32 = jnp.float32
BF16 = jnp.bfloat16

LN_EPS = 1e-5
HEAD_DIM = 64
LANES = 128
SUBLANES = 8
BF16_ROWS = 16
VMEM_LIMIT_BYTES = 56 * 1024 * 1024
NEG_BIG = -1e30
LOG2E = 1.4426950408889634

TOKEN_TILE = 512
ATTN_TILE = 256
ATTN_KEY_TILE = 512
CONV_ROW_CHUNK = 32
A_HIST = 32
B_HIST = 8


def _dot(a, b):
    return jnp.dot(a, b, preferred_element_type=F32)


def _dot_nt(a, b):
    return lax.dot_general(a, b, (((1,), (1,)), ((), ())), preferred_element_type=F32)


def _sigmoid(v):
    return 1.0 / (1.0 + jnp.exp(-v))


def _silu(v):
    return v * _sigmoid(v)


def _log_sigmoid(v):
    return jnp.minimum(v, 0.0) - jnp.log1p(jnp.exp(-jnp.abs(v)))


def _layer_norm(v, g, b):
    mu = jnp.mean(v, axis=-1, keepdims=True)
    c = v - mu
    var = jnp.mean(c * c, axis=-1, keepdims=True)
    return c * lax.rsqrt(var + LN_EPS) * g + b


def _split3(v):
    hi = v.astype(BF16)
    r1 = v - hi.astype(F32)
    mid = r1.astype(BF16)
    lo = (r1 - mid.astype(F32)).astype(BF16)
    return hi, mid, lo


def _const_spec(shape, layer):
    nd = len(shape)
    return pl.BlockSpec((None,) + tuple(shape), lambda *_, l=layer, nd=nd: (l,) + (0,) * nd,
                        pipeline_mode=pl.Buffered(1))


def _mem_kernel(mem_ref, g_ref, b_ref, wkT_ref, wv_ref, mkT_ref, mv_ref, *, n_heads):
    mem_n = _layer_norm(mem_ref[...], g_ref[...], b_ref[...]).astype(BF16)
    mkT = _dot_nt(wkT_ref[...], mem_n)
    mv = _dot(mem_n, wv_ref[...])
    row_head = lax.broadcasted_iota(jnp.int32, mkT.shape, 0) // HEAD_DIM
    col_head = lax.broadcasted_iota(jnp.int32, mv.shape, 1) // HEAD_DIM
    for h in range(n_heads):
        mkT_ref[h] = jnp.where(row_head == h, mkT, 0.0).astype(BF16)
        mv_ref[h] = jnp.where(col_head == h, mv, 0.0).astype(BF16)


def _mem_call(mem, g, b, wkT, wv):
    bsz, mlen, d = mem.shape
    depth, mw, _ = wkT.shape
    n_heads = mw // HEAD_DIM
    out = jax.ShapeDtypeStruct((bsz, depth, n_heads, mw, mlen), BF16), \
        jax.ShapeDtypeStruct((bsz, depth, n_heads, mlen, mw), BF16)
    return pl.pallas_call(
        functools.partial(_mem_kernel, n_heads=n_heads),
        grid=(bsz, depth),
        in_specs=[
            pl.BlockSpec((None, mlen, d), lambda i, l: (i, 0, 0)),
            pl.BlockSpec((1, d), lambda i, l: (0, 0)),
            pl.BlockSpec((1, d), lambda i, l: (0, 0)),
            pl.BlockSpec((None, mw, d), lambda i, l: (l, 0, 0)),
            pl.BlockSpec((None, d, mw), lambda i, l: (l, 0, 0)),
        ],
        out_specs=[
            pl.BlockSpec((None, None, n_heads, mw, mlen), lambda i, l: (i, l, 0, 0, 0)),
            pl.BlockSpec((None, None, n_heads, mlen, mw), lambda i, l: (i, l, 0, 0, 0)),
        ],
        out_shape=out,
        compiler_params=pltpu.CompilerParams(dimension_semantics=("arbitrary", "arbitrary"),
                                             vmem_limit_bytes=VMEM_LIMIT_BYTES),
        name="mem_kv",
    )(mem, g, b, wkT, wv)


def _proj_kernel(x_ref, wa_ref, wb_ref, wkc_ref, wqvT_ref, wm_ref, bfn_ref, bfT_ref,
                 caw_ref, cab_ref, lag_ref, lab_ref, cbw_ref, mkT_ref, mv_ref, tril_ref, triu_ref,
                 selk_ref, selq_ref,
                 ha_ref, hb_ref, ke_ref, sgc_ref, qte_ref, vT_ref, hm_ref,
                 abuf, ashift, aconv, bbuf, carry_n, carry_t,
                 *, tm, aw, bw, cw, mw, n_cheads, taps_a, taps_b):
    @pl.when(pl.program_id(1) == 0)
    def _():
        abuf[0:A_HIST, :] = jnp.zeros((A_HIST, aw), F32)
        bbuf[0:B_HIST, :] = jnp.zeros((B_HIST, bw), F32)
        carry_n[...] = jnp.zeros_like(carry_n)
        carry_t[...] = jnp.zeros_like(carry_t)

    xb = x_ref[...].astype(BF16)

    xm = _dot(xb, wm_ref[...])
    xa = _dot(xb, wa_ref[...])
    mq = (xm[:, :mw] * (HEAD_DIM ** -0.5)).astype(BF16)
    mem_scores = [_dot(mq, mkT_ref[h]) for h in range(mw // HEAD_DIM)]
    xbm = _dot(xb, wb_ref[...])
    xk = _dot(xb, wkc_ref[...])
    qv = _dot_nt(wqvT_ref[...], xb)

    abuf[A_HIST:A_HIST + tm, :] = xa[:, :aw] * _sigmoid(xa[:, aw:2 * aw])
    n_shift_rows = tm + A_HIST - SUBLANES
    for sh in range(1, SUBLANES):
        ashift[sh - 1] = abuf[sh:sh + n_shift_rows, :]
    first = A_HIST - (taps_a - 1)
    for r0 in range(0, tm, CONV_ROW_CHUNK):
        acc = jnp.broadcast_to(cab_ref[...], (CONV_ROW_CHUNK, aw))
        for t in range(taps_a):
            blk, sh = divmod(first + t, SUBLANES)
            start = r0 + blk * SUBLANES
            if sh == 0:
                rows = abuf[start:start + CONV_ROW_CHUNK, :]
            else:
                rows = ashift[sh - 1, start:start + CONV_ROW_CHUNK, :]
            acc = acc + jnp.tile(caw_ref[t], (CONV_ROW_CHUNK // SUBLANES, 1)) * rows
        aconv[r0:r0 + CONV_ROW_CHUNK, :] = acc
    abuf[0:A_HIST, :] = abuf[tm:tm + A_HIST, :]
    a = _silu(_layer_norm(aconv[...], lag_ref[...], lab_ref[...]))
    ha_ref[...] = (a * _silu(xa[:, 2 * aw:])).astype(BF16)

    om = jnp.zeros((tm, mw), F32)
    for h, s in enumerate(mem_scores):
        e = jnp.exp(s - jnp.max(s, axis=-1, keepdims=True))
        p = e / jnp.sum(e, axis=-1, keepdims=True)
        om = om + _dot(p.astype(BF16), mv_ref[h])
    hm_ref[...] = (om * _silu(xm[:, mw:])).astype(BF16)

    bbuf[B_HIST:B_HIST + tm, :] = xbm[:, 2 * bw:3 * bw] * xbm[:, :bw]
    firstb = B_HIST - (taps_b - 1)
    convb = cbw_ref[0:1, :] * bbuf[firstb:firstb + tm, :]
    for t in range(1, taps_b):
        convb = convb + cbw_ref[t:t + 1, :] * bbuf[firstb + t:firstb + t + tm, :]
    bbuf[0:B_HIST, :] = bbuf[tm:tm + B_HIST, :]
    hb_ref[...] = (xbm[:, bw:2 * bw] * convb * _silu(xbm[:, 3 * bw:])).astype(BF16)

    sgc_ref[...] = _silu(xk[:, cw:2 * cw]).astype(BF16)
    lf = _log_sigmoid(xk[:, 2 * cw:] + bfn_ref[...])
    hi, mid, lo = _split3(lf)
    tril = tril_ref[...]
    cum = _dot(tril, hi) + _dot(tril, mid) + _dot(tril, lo) + carry_n[...]
    carry_n[...] = cum[tm - 1:tm, :]
    parts = jnp.concatenate(_split3(cum * LOG2E), axis=1)
    lane = lax.broadcasted_iota(jnp.int32, (tm, LANES), 1)
    in_slab = lane & (BF16_ROWS - 1)
    ones_cols = jnp.where((in_slab >= 3) & (in_slab < 6) & (lane < BF16_ROWS * n_cheads), 1.0, 0.0)
    ke_ref[:, :cw] = xk[:, :cw].astype(BF16)
    ke_ref[:, cw:] = (_dot(parts, selk_ref[...]) + ones_cols).astype(BF16)

    vT_ref[...] = qv[cw:2 * cw].astype(BF16)
    lft = _log_sigmoid(qv[2 * cw:] + bfT_ref[...][:, 0:1])
    hi, mid, lo = _split3(lft)
    triu = triu_ref[...]
    cumt = _dot(hi, triu) + _dot(mid, triu) + _dot(lo, triu) + carry_t[...][:, 0:1]
    carry_t[...] = jnp.broadcast_to(cumt[:, tm - 1:tm], carry_t.shape)
    parts_t = jnp.concatenate(_split3(cumt * LOG2E), axis=0)
    sub = lax.broadcasted_iota(jnp.int32, (LANES, tm), 0)
    ones_rows = jnp.where(((sub & (BF16_ROWS - 1)) < 3) & (sub < BF16_ROWS * n_cheads), 1.0, 0.0)
    qte_ref[:cw, :] = (qv[:cw] * (LOG2E * HEAD_DIM ** -0.5)).astype(BF16)
    qte_ref[cw:, :] = (_dot(selq_ref[...], parts_t) + ones_rows).astype(BF16)


def _proj_call(x, layer, w, mkT, mv, tril, triu, selk, selq, dims):
    bsz, seq, d = x.shape
    tm = min(TOKEN_TILE, seq)
    aw, bw, cw, mw, nch = dims["aw"], dims["bw"], dims["cw"], dims["mw"], dims["n_cheads"]
    taps_a, taps_b = w["caw"].shape[1], w["cbw"].shape[1]
    n_heads_m = mw // HEAD_DIM
    mlen = mv.shape[3]
    row = lambda width: pl.BlockSpec((None, tm, width), lambda i, s: (i, s, 0))
    col = lambda height: pl.BlockSpec((None, height, tm), lambda i, s: (i, 0, s))
    in_specs = [
        row(d),
        _const_spec(w["wa"].shape[1:], layer), _const_spec(w["wb"].shape[1:], layer),
        _const_spec(w["wkc"].shape[1:], layer), _const_spec(w["wqvT"].shape[1:], layer),
        _const_spec(w["wm"].shape[1:], layer),
        _const_spec(w["bfn"].shape[1:], layer), _const_spec(w["bfT"].shape[1:], layer),
        _const_spec(w["caw"].shape[1:], layer), _const_spec(w["cab"].shape[1:], layer),
        _const_spec(w["lag"].shape[1:], layer), _const_spec(w["lab"].shape[1:], layer),
        _const_spec(w["cbw"].shape[1:], layer),
        pl.BlockSpec((None, None, n_heads_m, mw, mlen), lambda i, s, l=layer: (i, l, 0, 0, 0)),
        pl.BlockSpec((None, None, n_heads_m, mlen, mw), lambda i, s, l=layer: (i, l, 0, 0, 0)),
        pl.BlockSpec((tm, tm), lambda i, s: (0, 0), pipeline_mode=pl.Buffered(1)),
        pl.BlockSpec((tm, tm), lambda i, s: (0, 0), pipeline_mode=pl.Buffered(1)),
        pl.BlockSpec(selk.shape, lambda i, s: (0, 0), pipeline_mode=pl.Buffered(1)),
        pl.BlockSpec(selq.shape, lambda i, s: (0, 0), pipeline_mode=pl.Buffered(1)),
    ]
    out_specs = [row(aw), row(bw), row(cw + LANES), row(cw), col(cw + LANES), col(cw), row(mw)]
    out_shape = [
        jax.ShapeDtypeStruct((bsz, seq, aw), BF16), jax.ShapeDtypeStruct((bsz, seq, bw), BF16),
        jax.ShapeDtypeStruct((bsz, seq, cw + LANES), BF16), jax.ShapeDtypeStruct((bsz, seq, cw), BF16),
        jax.ShapeDtypeStruct((bsz, cw + LANES, seq), BF16), jax.ShapeDtypeStruct((bsz, cw, seq), BF16),
        jax.ShapeDtypeStruct((bsz, seq, mw), BF16),
    ]
    scratch = [
        pltpu.VMEM((A_HIST + tm, aw), F32),
        pltpu.VMEM((SUBLANES - 1, tm + A_HIST - SUBLANES, aw), F32),
        pltpu.VMEM((tm, aw), F32),
        pltpu.VMEM((B_HIST + tm, bw), F32),
        pltpu.VMEM((1, LANES), F32),
        pltpu.VMEM((BF16_ROWS, LANES), F32),
    ]
    kern = functools.partial(_proj_kernel, tm=tm, aw=aw, bw=bw, cw=cw, mw=mw, n_cheads=nch,
                             taps_a=taps_a, taps_b=taps_b)
    return pl.pallas_call(
        kern, grid=(bsz, seq // tm), in_specs=in_specs, out_specs=out_specs, out_shape=out_shape,
        scratch_shapes=scratch,
        compiler_params=pltpu.CompilerParams(dimension_semantics=("arbitrary", "arbitrary"),
                                             vmem_limit_bytes=VMEM_LIMIT_BYTES),
        name="proj_branches",
    )(x, w["wa"], w["wb"], w["wkc"], w["wqvT"], w["wm"], w["bfn"], w["bfT"],
      w["caw"], w["cab"], w["lag"], w["lab"], w["cbw"], mkT, mv, tril, triu, selk, selq)


def _attn_kernel(qte_ref, ke_ref, vT_ref, sg_ref, o_ref, qh_ref, m_ref, acc_ref, sa_ref, sb_ref,
                 *, tq, tk, cw, n_heads):
    i = pl.program_id(1)
    n_diag = max(tq // tk, 1)
    n_before = (i * tq) // tk
    tile = tq
    pair_w = 2 * HEAD_DIM
    ones_rows = jnp.ones((BF16_ROWS, tk), BF16)
    key_pos = lax.broadcasted_iota(jnp.int32, (tk, tq), 0) + n_before * tk
    query_pos = lax.broadcasted_iota(jnp.int32, (tk, tq), 1) + i * tq
    causal = [key_pos + d * tk <= query_pos for d in range(n_diag)]

    def build_query_operands():
        feat_row = lax.broadcasted_iota(jnp.int32, (pair_w, tile), 0)
        for h in range(n_heads):
            pair, hh = divmod(h, 2)
            qpair = qte_ref[pair * pair_w:(pair + 1) * pair_w, :]
            own = (feat_row >= hh * HEAD_DIM) & (feat_row < (hh + 1) * HEAD_DIM)
            qh_ref[h, :pair_w, :] = jnp.where(own, qpair, jnp.zeros_like(qpair))
            qh_ref[h, pair_w:, :] = jnp.zeros((LANES, tile), BF16)
            slab = slice(BF16_ROWS * h, BF16_ROWS * (h + 1))
            qh_ref[h, pair_w + slab.start:pair_w + slab.stop, :] = qte_ref[cw + slab.start:cw + slab.stop, :]

    def set_reference(shift):
        slab_row = lax.broadcasted_iota(jnp.int32, (BF16_ROWS, tile), 0)
        for h in range(n_heads):
            lo_row = cw + BF16_ROWS * h
            slab = qte_ref[lo_row:lo_row + BF16_ROWS, :].astype(F32)
            parts = _split3(slab[3:4, :] + slab[4:5, :] + slab[5:6, :] - shift[h])
            new = jnp.where(slab_row < 3, 1.0, 0.0)
            for r in range(3):
                new = jnp.where(slab_row == 3 + r, parts[r].astype(F32), new)
            qh_ref[h, pair_w + BF16_ROWS * h:pair_w + BF16_ROWS * (h + 1), :] = new.astype(BF16)

    def value_operand(h, start):
        return jnp.concatenate([vT_ref[h * HEAD_DIM:(h + 1) * HEAD_DIM, pl.ds(start, tk)], ones_rows], axis=0)

    def score_head(s_ref, j, h):
        start = pl.multiple_of(j * tk, tk)
        pair = h // 2
        kblk = jnp.concatenate([ke_ref[pl.ds(start, tk), pair * pair_w:(pair + 1) * pair_w],
                                ke_ref[pl.ds(start, tk), cw:]], axis=1)
        s_ref[h] = _dot(kblk, qh_ref[h])

    def scores_into(s_ref, j):
        for h in range(n_heads):
            score_head(s_ref, j, h)

    def absorb_head(s_ref, j, h, mask=None):
        start = pl.multiple_of(j * tk, tk)
        s = s_ref[h]
        if mask is not None:
            s = jnp.where(mask, s, NEG_BIG)
        m_old = m_ref[h]
        m_new = jnp.maximum(m_old, jnp.max(s, axis=0, keepdims=True))
        p = jnp.exp2(s - m_new)
        alpha = jnp.exp2(m_old - m_new)
        m_ref[h] = m_new
        acc_ref[h] = alpha * acc_ref[h] + _dot(value_operand(h, start), p.astype(BF16))

    def absorb_referenced_head(s_ref, j, h):
        start = pl.multiple_of(j * tk, tk)
        acc_ref[h] = acc_ref[h] + _dot(value_operand(h, start), jnp.exp2(s_ref[h]).astype(BF16))

    def absorb(s_ref, j, absorb_fn, **kw):
        for h in range(n_heads):
            absorb_fn(s_ref, j, h, **kw)

    def pipelined(n_blocks, absorb_fn):
        def advance(nxt_ref, j_next, cur_ref, j_cur):
            for h in range(n_heads):
                score_head(nxt_ref, j_next, h)
                absorb_fn(cur_ref, j_cur, h)

        def two_blocks(jj, carry):
            j = 2 * jj
            advance(sb_ref, j + 1, sa_ref, j)
            advance(sa_ref, j + 2, sb_ref, j + 1)
            return carry

        @pl.when(n_blocks > 0)
        def _():
            scores_into(sa_ref, 0)

        lax.fori_loop(0, jnp.maximum(n_blocks - 1, 0) // 2, two_blocks, 0)

        @pl.when(n_blocks % 2 == 1)
        def _():
            absorb(sa_ref, n_blocks - 1, absorb_fn)

        @pl.when((n_blocks % 2 == 0) & (n_blocks > 0))
        def _():
            advance(sb_ref, n_blocks - 1, sa_ref, n_blocks - 2)
            absorb(sb_ref, n_blocks - 1, absorb_fn)

    assert n_diag <= 2
    diag_refs = (sa_ref, sb_ref)[:n_diag]
    build_query_operands()
    for d, s_ref in enumerate(diag_refs):
        scores_into(s_ref, n_before + d)
    m0 = []
    for h in range(n_heads):
        tiles = [jnp.where(causal[d], s_ref[h], NEG_BIG) for d, s_ref in enumerate(diag_refs)]
        m = jnp.max(tiles[0], axis=0, keepdims=True)
        for s in tiles[1:]:
            m = jnp.maximum(m, jnp.max(s, axis=0, keepdims=True))
        total = None
        for d, s in enumerate(tiles):
            start = pl.multiple_of((n_before + d) * tk, tk)
            part = _dot(value_operand(h, start), jnp.exp2(s - m).astype(BF16))
            total = part if total is None else total + part
        acc_ref[h] = total
        m0.append(m)
    set_reference(m0)
    pipelined(n_before, absorb_referenced_head)

    worst = jnp.max(jnp.abs(acc_ref[...].reshape(n_heads * (HEAD_DIM + BF16_ROWS), tile)))

    @pl.when(jnp.logical_not(worst < jnp.finfo(F32).max))
    def _():
        build_query_operands()
        m_ref[...] = jnp.full(m_ref.shape, NEG_BIG, F32)
        acc_ref[...] = jnp.zeros(acc_ref.shape, F32)
        pipelined(n_before, absorb_head)
        for d, s_ref in enumerate(diag_refs):
            scores_into(s_ref, n_before + d)
        for d, s_ref in enumerate(diag_refs):
            absorb(s_ref, n_before + d, absorb_head, mask=causal[d])

    for pair in range(n_heads // 2):
        oT = jnp.concatenate([acc_ref[h, :HEAD_DIM, :] / acc_ref[h, HEAD_DIM:HEAD_DIM + 1, :]
                              for h in (2 * pair, 2 * pair + 1)], axis=0)
        cols = slice(pair * pair_w, (pair + 1) * pair_w)
        o_ref[:, cols] = (oT.T * sg_ref[:, cols].astype(F32)).astype(BF16)


def _attn_call(qte, ke, vT, sg):
    bsz, cw, seq = vT.shape
    tile = min(ATTN_TILE, seq)
    tk = min(ATTN_KEY_TILE, seq)
    assert seq % tile == 0 and seq % tk == 0 and (tile % tk == 0 or tk % tile == 0)
    n_heads = cw // HEAD_DIM
    kw = cw + LANES
    return pl.pallas_call(
        functools.partial(_attn_kernel, tq=tile, tk=tk, cw=cw, n_heads=n_heads),
        grid=(bsz, seq // tile),
        in_specs=[
            pl.BlockSpec((None, kw, tile), lambda b, i: (b, 0, i)),
            pl.BlockSpec((None, seq, kw), lambda b, i: (b, 0, 0)),
            pl.BlockSpec((None, cw, seq), lambda b, i: (b, 0, 0)),
            pl.BlockSpec((None, tile, cw), lambda b, i: (b, i, 0)),
        ],
        out_specs=pl.BlockSpec((None, tile, cw), lambda b, i: (b, i, 0)),
        out_shape=jax.ShapeDtypeStruct((bsz, seq, cw), BF16),
        scratch_shapes=[
            pltpu.VMEM((n_heads, 2 * HEAD_DIM + LANES, tile), BF16),
            pltpu.VMEM((n_heads, 1, tile), F32),
            pltpu.VMEM((n_heads, HEAD_DIM + BF16_ROWS, tile), F32),
            pltpu.VMEM((n_heads, tk, tile), F32),
            pltpu.VMEM((n_heads, tk, tile), F32),
        ],
        compiler_params=pltpu.CompilerParams(dimension_semantics=("arbitrary", "arbitrary"),
                                             vmem_limit_bytes=VMEM_LIMIT_BYTES),
        name="forget_attn",
    )(qte, ke, vT, sg)


def _merge_kernel(x_ref, ha_ref, hb_ref, hc_ref, hm_ref, wg_ref, pa_ref, pb_ref, pc_ref, pm_ref,
                  wo_ref, g_ref, b_ref, o_ref, *, alpha):
    x = x_ref[...]
    xb = x.astype(BF16)
    d = x.shape[-1]
    merged = None
    for n, (h_ref, p_ref) in enumerate(((ha_ref, pa_ref), (hb_ref, pb_ref), (hc_ref, pc_ref), (hm_ref, pm_ref))):
        gate = _sigmoid(_dot(xb, wg_ref[:, n * d:(n + 1) * d]))
        term = gate * _dot(h_ref[...], p_ref[...])
        merged = term if merged is None else merged + term
    out = _dot(merged.astype(BF16), wo_ref[...])
    o_ref[...] = _layer_norm(alpha * x + out, g_ref[...], b_ref[...])


def _merge_call(x, ha, hb, hc, hm, layer, w, alpha):
    bsz, seq, d = x.shape
    tm = min(TOKEN_TILE, seq)
    row = lambda width: pl.BlockSpec((None, tm, width), lambda i, s: (i, s, 0))
    in_specs = [row(d), row(ha.shape[-1]), row(hb.shape[-1]), row(hc.shape[-1]), row(hm.shape[-1])]
    names = ("wg", "pa", "pb", "pc", "pm", "wo", "lng", "lnb")
    in_specs += [_const_spec(w[n].shape[1:], layer) for n in names]
    return pl.pallas_call(
        functools.partial(_merge_kernel, alpha=alpha),
        grid=(bsz, seq // tm), in_specs=in_specs, out_specs=row(d),
        out_shape=jax.ShapeDtypeStruct((bsz, seq, d), F32),
        compiler_params=pltpu.CompilerParams(dimension_semantics=("arbitrary", "arbitrary"),
                                             vmem_limit_bytes=VMEM_LIMIT_BYTES),
        name="merge_out_norm",
    )(x, ha, hb, hc, hm, *[w[n] for n in names])


def _prepare_weights(w_in, b_forget, conv_a_w, conv_a_b, ln_a_g, ln_a_b, conv_b_w, p_a, p_b, p_c, p_m,
                     w_out, ln_g, ln_b, dims):
    aw, bw, cw, mw, nch, d = dims["aw"], dims["bw"], dims["cw"], dims["mw"], dims["n_cheads"], dims["d"]
    depth = w_in.shape[0]
    cuts = [0]
    for width in (3 * aw, 4 * bw, cw, cw, cw, nch, cw, 2 * mw, 4 * d):
        cuts.append(cuts[-1] + width)
    assert cuts[-1] == w_in.shape[-1]
    seg = lambda n: w_in[:, :, cuts[n]:cuts[n + 1]]
    wq, wk, wv, wf, wcg = seg(2), seg(3), seg(4), seg(5), seg(6)
    wf_cols = jnp.pad(wf, ((0, 0), (0, 0), (0, LANES - nch)))
    wf_rows = jnp.pad(jnp.swapaxes(wf, 1, 2), ((0, 0), (0, BF16_ROWS - nch), (0, 0)))
    bf_pad_n = jnp.pad(b_forget, ((0, 0), (0, LANES - nch)))[:, None, :]
    bf_pad_t = jnp.broadcast_to(jnp.pad(b_forget, ((0, 0), (0, BF16_ROWS - nch)))[:, :, None],
                                (depth, BF16_ROWS, LANES))
    return {
        "wa": seg(0).astype(BF16), "wb": seg(1).astype(BF16),
        "wkc": jnp.concatenate([wk, wcg, wf_cols], axis=-1).astype(BF16),
        "wqvT": jnp.concatenate([jnp.swapaxes(wq, 1, 2), jnp.swapaxes(wv, 1, 2), wf_rows], axis=1).astype(BF16),
        "wm": seg(7).astype(BF16), "wg": seg(8).astype(BF16),
        "bfn": bf_pad_n, "bfT": bf_pad_t,
        "caw": jnp.broadcast_to(conv_a_w[:, :, None, :], conv_a_w.shape[:2] + (SUBLANES, aw)),
        "cab": conv_a_b[:, None, :], "lag": ln_a_g[:, None, :], "lab": ln_a_b[:, None, :],
        "cbw": conv_b_w,
        "pa": p_a.astype(BF16), "pb": p_b.astype(BF16), "pc": p_c.astype(BF16), "pm": p_m.astype(BF16),
        "wo": w_out.astype(BF16), "lng": ln_g[:, None, :], "lnb": ln_b[:, None, :],
    }


def _decay_selectors(n_heads):
    assert BF16_ROWS * n_heads <= LANES and n_heads <= BF16_ROWS
    selk = np.zeros((3 * LANES, LANES), np.float32)
    selq = np.zeros((LANES, 3 * BF16_ROWS), np.float32)
    for h in range(n_heads):
        for r in range(3):
            selk[r * LANES + h, BF16_ROWS * h + r] = -1.0
            selq[BF16_ROWS * h + 3 + r, r * BF16_ROWS + h] = 1.0
    return jnp.asarray(selk, BF16), jnp.asarray(selq, BF16)


def kernel(x, mem, w_in, b_forget, conv_a_w, conv_a_b, ln_a_g, ln_a_b, conv_b_w, w_kv_mem, mem_ln_g, mem_ln_b,
           p_a, p_b, p_c, p_m, w_out, ln_g, ln_b):
    bsz, seq, d = x.shape
    depth = w_in.shape[0]
    dims = {"d": d, "aw": conv_a_w.shape[-1], "bw": conv_b_w.shape[-1], "cw": p_c.shape[1], "mw": p_m.shape[1],
            "n_cheads": b_forget.shape[-1]}
    assert dims["cw"] == dims["n_cheads"] * HEAD_DIM and dims["n_cheads"] % 2 == 0
    assert dims["mw"] % HEAD_DIM == 0 and seq % min(TOKEN_TILE, seq) == 0 and seq % min(ATTN_TILE, seq) == 0
    assert conv_a_w.shape[1] - 1 <= A_HIST and conv_b_w.shape[1] - 1 <= B_HIST
    alpha = (2.0 * depth) ** 0.25
    w = _prepare_weights(w_in, b_forget, conv_a_w, conv_a_b, ln_a_g, ln_a_b, conv_b_w, p_a, p_b, p_c, p_m,
                         w_out, ln_g, ln_b, dims)
    mw = dims["mw"]
    wkT_mem = jnp.swapaxes(w_kv_mem[:, :, :mw], 1, 2).astype(BF16)
    wv_mem = w_kv_mem[:, :, mw:].astype(BF16)
    mkT, mv = _mem_call(mem, mem_ln_g[None, :], mem_ln_b[None, :], wkT_mem, wv_mem)
    tm = min(TOKEN_TILE, seq)
    tril = jnp.tril(jnp.ones((tm, tm), BF16))
    triu = jnp.triu(jnp.ones((tm, tm), BF16))
    selk, selq = _decay_selectors(dims["n_cheads"])
    for layer in range(depth):
        ha, hb, ke, sgc, qte, vT, hm = _proj_call(x, layer, w, mkT, mv, tril, triu, selk, selq, dims)
        hc = _attn_call(qte, ke, vT, sgc)
        x = _merge_call(x, ha, hb, hc, hm, layer, w, alpha)
    return x
```

```python
import functools

import jax
import jax.numpy as jnp
import numpy as np
from jax import lax
from jax.experimental import pallas as pl
from jax.experimental.pallas import tpu as pltpu

F32 = jnp.float32
BF16 = jnp.bfloat16

LN_EPS = 1e-5
HEAD_DIM = 64
LANES = 128
SUBLANES = 8
BF16_ROWS = 16
VMEM_LIMIT_BYTES = 56 * 1024 * 1024
NEG_BIG = -1e30
LOG2E = 1.4426950408889634

TOKEN_TILE = 512
ATTN_TILE = 256
ATTN_KEY_TILE = 512
CONV_TAPS_PER_PASS = 4
A_HIST = 32
B_HIST = 8


def _dot(a, b):
    return jnp.dot(a, b, preferred_element_type=F32)


def _dot_nt(a, b):
    return lax.dot_general(a, b, (((1,), (1,)), ((), ())), preferred_element_type=F32)


def _sigmoid(v):
    return 1.0 / (1.0 + jnp.exp(-v))


def _silu(v):
    return v * _sigmoid(v)


def _log_sigmoid(v):
    return jnp.minimum(v, 0.0) - jnp.log1p(jnp.exp(-jnp.abs(v)))


def _layer_norm(v, g, b):
    mu = jnp.mean(v, axis=-1, keepdims=True)
    c = v - mu
    var = jnp.mean(c * c, axis=-1, keepdims=True)
    return c * lax.rsqrt(var + LN_EPS) * g + b


def _split3(v):
    hi = v.astype(BF16)
    r1 = v - hi.astype(F32)
    mid = r1.astype(BF16)
    lo = (r1 - mid.astype(F32)).astype(BF16)
    return hi, mid, lo


def _const_spec(shape, layer):
    nd = len(shape)
    return pl.BlockSpec((None,) + tuple(shape), lambda *_, l=layer, nd=nd: (l,) + (0,) * nd,
                        pipeline_mode=pl.Buffered(1))


def _mem_kernel(mem_ref, g_ref, b_ref, wkT_ref, wv_ref, mkT_ref, mv_ref, *, n_heads):
    mem_n = _layer_norm(mem_ref[...], g_ref[...], b_ref[...]).astype(BF16)
    mkT = _dot_nt(wkT_ref[...], mem_n)
    mv = _dot(mem_n, wv_ref[...])
    row_head = lax.broadcasted_iota(jnp.int32, mkT.shape, 0) // HEAD_DIM
    col_head = lax.broadcasted_iota(jnp.int32, mv.shape, 1) // HEAD_DIM
    for h in range(n_heads):
        mkT_ref[h] = jnp.where(row_head == h, mkT, 0.0).astype(BF16)
        mv_ref[h] = jnp.where(col_head == h, mv, 0.0).astype(BF16)


def _mem_call(mem, g, b, wkT, wv):
    bsz, mlen, d = mem.shape
    depth, mw, _ = wkT.shape
    n_heads = mw // HEAD_DIM
    out = jax.ShapeDtypeStruct((bsz, depth, n_heads, mw, mlen), BF16), \
        jax.ShapeDtypeStruct((bsz, depth, n_heads, mlen, mw), BF16)
    return pl.pallas_call(
        functools.partial(_mem_kernel, n_heads=n_heads),
        grid=(bsz, depth),
        in_specs=[
            pl.BlockSpec((None, mlen, d), lambda i, l: (i, 0, 0)),
            pl.BlockSpec((1, d), lambda i, l: (0, 0)),
            pl.BlockSpec((1, d), lambda i, l: (0, 0)),
            pl.BlockSpec((None, mw, d), lambda i, l: (l, 0, 0)),
            pl.BlockSpec((None, d, mw), lambda i, l: (l, 0, 0)),
        ],
        out_specs=[
            pl.BlockSpec((None, None, n_heads, mw, mlen), lambda i, l: (i, l, 0, 0, 0)),
            pl.BlockSpec((None, None, n_heads, mlen, mw), lambda i, l: (i, l, 0, 0, 0)),
        ],
        out_shape=out,
        compiler_params=pltpu.CompilerParams(dimension_semantics=("arbitrary", "arbitrary"),
                                             vmem_limit_bytes=VMEM_LIMIT_BYTES),
        name="mem_kv",
    )(mem, g, b, wkT, wv)


def _proj_kernel(x_ref, wa_ref, wb_ref, wkc_ref, wqvT_ref, wm_ref, bfn_ref, bfT_ref,
                 caw_ref, cab_ref, lag_ref, lab_ref, cbw_ref, mkT_ref, mv_ref, tril_ref, triu_ref,
                 selk_ref, selq_ref,
                 ha_ref, hb_ref, ke_ref, sgc_ref, qte_ref, vT_ref, hm_ref,
                 abuf, ashift, aconv, bbuf, carry_n, carry_t,
                 *, tm, aw, bw, cw, mw, n_cheads, taps_a, taps_b):
    @pl.when(pl.program_id(1) == 0)
    def _():
        abuf[0:A_HIST, :] = jnp.zeros((A_HIST, aw), F32)
        bbuf[0:B_HIST, :] = jnp.zeros((B_HIST, bw), F32)
        carry_n[...] = jnp.zeros_like(carry_n)
        carry_t[...] = jnp.zeros_like(carry_t)

    xb = x_ref[...].astype(BF16)

    xm = _dot(xb, wm_ref[...])
    xa = _dot(xb, wa_ref[...])
    mq = (xm[:, :mw] * (HEAD_DIM ** -0.5)).astype(BF16)
    mem_scores = [_dot(mq, mkT_ref[h]) for h in range(mw // HEAD_DIM)]
    xbm = _dot(xb, wb_ref[...])
    xk = _dot(xb, wkc_ref[...])
    qv = _dot_nt(wqvT_ref[...], xb)

    abuf[A_HIST:A_HIST + tm, :] = xa[:, :aw] * _sigmoid(xa[:, aw:2 * aw])
    n_shift_rows = tm + A_HIST - SUBLANES
    for sh in range(1, SUBLANES):
        ashift[sh - 1] = abuf[sh:sh + n_shift_rows, :]
    first = A_HIST - (taps_a - 1)
    for t0 in range(0, taps_a, CONV_TAPS_PER_PASS):
        acc = jnp.broadcast_to(cab_ref[...], (tm, aw)) if t0 == 0 else aconv[...]
        for t in range(t0, min(t0 + CONV_TAPS_PER_PASS, taps_a)):
            blk, sh = divmod(first + t, SUBLANES)
            start = blk * SUBLANES
            rows = abuf[start:start + tm, :] if sh == 0 else ashift[sh - 1, start:start + tm, :]
            acc = acc + jnp.tile(caw_ref[t], (tm // SUBLANES, 1)) * rows
        aconv[...] = acc
    abuf[0:A_HIST, :] = abuf[tm:tm + A_HIST, :]
    a = _silu(_layer_norm(aconv[...], lag_ref[...], lab_ref[...]))
    ha_ref[...] = (a * _silu(xa[:, 2 * aw:])).astype(BF16)

    om = jnp.zeros((tm, mw), F32)
    for h, s in enumerate(mem_scores):
        e = jnp.exp(s - jnp.max(s, axis=-1, keepdims=True))
        p = e * (1.0 / jnp.sum(e, axis=-1, keepdims=True))
        om = om + _dot(p.astype(BF16), mv_ref[h])
    hm_ref[...] = (om * _silu(xm[:, mw:])).astype(BF16)

    bbuf[B_HIST:B_HIST + tm, :] = xbm[:, 2 * bw:3 * bw] * xbm[:, :bw]
    firstb = B_HIST - (taps_b - 1)
    convb = cbw_ref[0:1, :] * bbuf[firstb:firstb + tm, :]
    for t in range(1, taps_b):
        convb = convb + cbw_ref[t:t + 1, :] * bbuf[firstb + t:firstb + t + tm, :]
    bbuf[0:B_HIST, :] = bbuf[tm:tm + B_HIST, :]
    hb_ref[...] = (xbm[:, bw:2 * bw] * convb * _silu(xbm[:, 3 * bw:])).astype(BF16)

    sgc_ref[...] = _silu(xk[:, cw:2 * cw]).astype(BF16)
    lf = _log_sigmoid(xk[:, 2 * cw:] + bfn_ref[...])
    cums = _dot(tril_ref[...], jnp.concatenate(_split3(lf), axis=1))
    cum = cums[:, :LANES] + cums[:, LANES:2 * LANES] + cums[:, 2 * LANES:] + carry_n[...]
    carry_n[...] = cum[tm - 1:tm, :]
    parts = jnp.concatenate(_split3(cum * LOG2E), axis=1)
    lane = lax.broadcasted_iota(jnp.int32, (tm, LANES), 1)
    in_slab = lane & (BF16_ROWS - 1)
    ones_cols = jnp.where((in_slab >= 3) & (in_slab < 6) & (lane < BF16_ROWS * n_cheads), 1.0, 0.0)
    ke_ref[:, :cw] = xk[:, :cw].astype(BF16)
    ke_ref[:, cw:] = (_dot(parts, selk_ref[...]) + ones_cols).astype(BF16)

    vT_ref[...] = qv[cw:2 * cw].astype(BF16)
    lft = _log_sigmoid(qv[2 * cw:] + bfT_ref[...][:, 0:1])
    cumts = _dot(jnp.concatenate(_split3(lft), axis=0), triu_ref[...])
    cumt = cumts[:BF16_ROWS] + cumts[BF16_ROWS:2 * BF16_ROWS] + cumts[2 * BF16_ROWS:] + carry_t[...][:, 0:1]
    carry_t[...] = jnp.broadcast_to(cumt[:, tm - 1:tm], carry_t.shape)
    parts_t = jnp.concatenate(_split3(cumt * LOG2E), axis=0)
    sub = lax.broadcasted_iota(jnp.int32, (LANES, tm), 0)
    ones_rows = jnp.where(((sub & (BF16_ROWS - 1)) < 3) & (sub < BF16_ROWS * n_cheads), 1.0, 0.0)
    qte_ref[:cw, :] = (qv[:cw] * (LOG2E * HEAD_DIM ** -0.5)).astype(BF16)
    qte_ref[cw:, :] = (_dot(selq_ref[...], parts_t) + ones_rows).astype(BF16)


def _proj_call(x, layer, w, mkT, mv, tril, triu, selk, selq, dims):
    bsz, seq, d = x.shape
    tm = min(TOKEN_TILE, seq)
    aw, bw, cw, mw, nch = dims["aw"], dims["bw"], dims["cw"], dims["mw"], dims["n_cheads"]
    taps_a, taps_b = w["caw"].shape[1], w["cbw"].shape[1]
    n_heads_m = mw // HEAD_DIM
    mlen = mv.shape[3]
    row = lambda width: pl.BlockSpec((None, tm, width), lambda i, s: (i, s, 0))
    col = lambda height: pl.BlockSpec((None, height, tm), lambda i, s: (i, 0, s))
    in_specs = [
        row(d),
        _const_spec(w["wa"].shape[1:], layer), _const_spec(w["wb"].shape[1:], layer),
        _const_spec(w["wkc"].shape[1:], layer), _const_spec(w["wqvT"].shape[1:], layer),
        _const_spec(w["wm"].shape[1:], layer),
        _const_spec(w["bfn"].shape[1:], layer), _const_spec(w["bfT"].shape[1:], layer),
        _const_spec(w["caw"].shape[1:], layer), _const_spec(w["cab"].shape[1:], layer),
        _const_spec(w["lag"].shape[1:], layer), _const_spec(w["lab"].shape[1:], layer),
        _const_spec(w["cbw"].shape[1:], layer),
        pl.BlockSpec((None, None, n_heads_m, mw, mlen), lambda i, s, l=layer: (i, l, 0, 0, 0)),
        pl.BlockSpec((None, None, n_heads_m, mlen, mw), lambda i, s, l=layer: (i, l, 0, 0, 0)),
        pl.BlockSpec((tm, tm), lambda i, s: (0, 0), pipeline_mode=pl.Buffered(1)),
        pl.BlockSpec((tm, tm), lambda i, s: (0, 0), pipeline_mode=pl.Buffered(1)),
        pl.BlockSpec(selk.shape, lambda i, s: (0, 0), pipeline_mode=pl.Buffered(1)),
        pl.BlockSpec(selq.shape, lambda i, s: (0, 0), pipeline_mode=pl.Buffered(1)),
    ]
    out_specs = [row(aw), row(bw), row(cw + LANES), row(cw), col(cw + LANES), col(cw), row(mw)]
    out_shape = [
        jax.ShapeDtypeStruct((bsz, seq, aw), BF16), jax.ShapeDtypeStruct((bsz, seq, bw), BF16),
        jax.ShapeDtypeStruct((bsz, seq, cw + LANES), BF16), jax.ShapeDtypeStruct((bsz, seq, cw), BF16),
        jax.ShapeDtypeStruct((bsz, cw + LANES, seq), BF16), jax.ShapeDtypeStruct((bsz, cw, seq), BF16),
        jax.ShapeDtypeStruct((bsz, seq, mw), BF16),
    ]
    scratch = [
        pltpu.VMEM((A_HIST + tm, aw), F32),
        pltpu.VMEM((SUBLANES - 1, tm + A_HIST - SUBLANES, aw), F32),
        pltpu.VMEM((tm, aw), F32),
        pltpu.VMEM((B_HIST + tm, bw), F32),
        pltpu.VMEM((1, LANES), F32),
        pltpu.VMEM((BF16_ROWS, LANES), F32),
    ]
    kern = functools.partial(_proj_kernel, tm=tm, aw=aw, bw=bw, cw=cw, mw=mw, n_cheads=nch,
                             taps_a=taps_a, taps_b=taps_b)
    return pl.pallas_call(
        kern, grid=(bsz, seq // tm), in_specs=in_specs, out_specs=out_specs, out_shape=out_shape,
        scratch_shapes=scratch,
        compiler_params=pltpu.CompilerParams(dimension_semantics=("arbitrary", "arbitrary"),
                                             vmem_limit_bytes=VMEM_LIMIT_BYTES),
        name="proj_branches",
    )(x, w["wa"], w["wb"], w["wkc"], w["wqvT"], w["wm"], w["bfn"], w["bfT"],
      w["caw"], w["cab"], w["lag"], w["lab"], w["cbw"], mkT, mv, tril, triu, selk, selq)


def _attn_kernel(qte_ref, ke_ref, vT_ref, sg_ref, o_ref, qh_ref, m_ref, acc_ref, sa_ref, sb_ref,
                 *, tq, tk, cw, n_heads):
    i = pl.program_id(1)
    n_diag = max(tq // tk, 1)
    n_before = (i * tq) // tk
    tile = tq
    pair_w = 2 * HEAD_DIM
    ones_rows = jnp.ones((BF16_ROWS, tk), BF16)
    key_pos = lax.broadcasted_iota(jnp.int32, (tk, tq), 0) + n_before * tk
    query_pos = lax.broadcasted_iota(jnp.int32, (tk, tq), 1) + i * tq
    causal = [key_pos + d * tk <= query_pos for d in range(n_diag)]

    def build_query_operands():
        feat_row = lax.broadcasted_iota(jnp.int32, (pair_w, tile), 0)
        for h in range(n_heads):
            pair, hh = divmod(h, 2)
            qpair = qte_ref[pair * pair_w:(pair + 1) * pair_w, :]
            own = (feat_row >= hh * HEAD_DIM) & (feat_row < (hh + 1) * HEAD_DIM)
            qh_ref[h, :pair_w, :] = jnp.where(own, qpair, jnp.zeros_like(qpair))
            qh_ref[h, pair_w:, :] = jnp.zeros((LANES, tile), BF16)
            slab = slice(BF16_ROWS * h, BF16_ROWS * (h + 1))
            qh_ref[h, pair_w + slab.start:pair_w + slab.stop, :] = qte_ref[cw + slab.start:cw + slab.stop, :]

    def set_reference(shift):
        slab_row = lax.broadcasted_iota(jnp.int32, (BF16_ROWS, tile), 0)
        for h in range(n_heads):
            lo_row = cw + BF16_ROWS * h
            slab = qte_ref[lo_row:lo_row + BF16_ROWS, :].astype(F32)
            parts = _split3(slab[3:4, :] + slab[4:5, :] + slab[5:6, :] - shift[h])
            new = jnp.where(slab_row < 3, 1.0, 0.0)
            for r in range(3):
                new = jnp.where(slab_row == 3 + r, parts[r].astype(F32), new)
            qh_ref[h, pair_w + BF16_ROWS * h:pair_w + BF16_ROWS * (h + 1), :] = new.astype(BF16)

    def value_operand(h, start):
        return jnp.concatenate([vT_ref[h * HEAD_DIM:(h + 1) * HEAD_DIM, pl.ds(start, tk)], ones_rows], axis=0)

    def score_head(s_ref, j, h):
        start = pl.multiple_of(j * tk, tk)
        pair = h // 2
        kblk = jnp.concatenate([ke_ref[pl.ds(start, tk), pair * pair_w:(pair + 1) * pair_w],
                                ke_ref[pl.ds(start, tk), cw:]], axis=1)
        s_ref[h] = _dot(kblk, qh_ref[h])

    def scores_into(s_ref, j):
        for h in range(n_heads):
            score_head(s_ref, j, h)

    def absorb_head(s_ref, j, h, mask=None):
        start = pl.multiple_of(j * tk, tk)
        s = s_ref[h]
        if mask is not None:
            s = jnp.where(mask, s, NEG_BIG)
        m_old = m_ref[h]
        m_new = jnp.maximum(m_old, jnp.max(s, axis=0, keepdims=True))
        p = jnp.exp2(s - m_new)
        alpha = jnp.exp2(m_old - m_new)
        m_ref[h] = m_new
        acc_ref[h] = alpha * acc_ref[h] + _dot(value_operand(h, start), p.astype(BF16))

    def absorb_referenced_head(s_ref, j, h):
        start = pl.multiple_of(j * tk, tk)
        acc_ref[h] = acc_ref[h] + _dot(value_operand(h, start), jnp.exp2(s_ref[h]).astype(BF16))

    def absorb(s_ref, j, absorb_fn, **kw):
        for h in range(n_heads):
            absorb_fn(s_ref, j, h, **kw)

    def advance(nxt_ref, j_next, cur_ref, j_cur, absorb_fn):
        for h in range(n_heads):
            score_head(nxt_ref, j_next, h)
            absorb_fn(cur_ref, j_cur, h)

    def pipelined(lo, hi, absorb_fn, preloaded=False):
        n_blocks = jnp.maximum(hi - lo, 0)

        def two_blocks(jj, carry):
            j = lo + 2 * jj
            advance(sb_ref, j + 1, sa_ref, j, absorb_fn)
            advance(sa_ref, j + 2, sb_ref, j + 1, absorb_fn)
            return carry

        if not preloaded:
            @pl.when(n_blocks > 0)
            def _():
                scores_into(sa_ref, lo)

        lax.fori_loop(0, jnp.maximum(n_blocks - 1, 0) // 2, two_blocks, 0)

        @pl.when(n_blocks % 2 == 1)
        def _():
            absorb(sa_ref, hi - 1, absorb_fn)

        @pl.when((n_blocks % 2 == 0) & (n_blocks > 0))
        def _():
            advance(sb_ref, hi - 1, sa_ref, hi - 2, absorb_fn)
            absorb(sb_ref, hi - 1, absorb_fn)

    assert n_diag <= 2
    diag_refs = (sa_ref, sb_ref)[:n_diag]
    early_first = n_diag == 1
    build_query_operands()
    for d, s_ref in enumerate(diag_refs):
        scores_into(s_ref, n_before + d)
    if early_first:
        scores_into(sb_ref, 0)
    m0 = []
    for h in range(n_heads):
        tiles = [jnp.where(causal[d], s_ref[h], NEG_BIG) for d, s_ref in enumerate(diag_refs)]
        m = jnp.max(tiles[0], axis=0, keepdims=True)
        for s in tiles[1:]:
            m = jnp.maximum(m, jnp.max(s, axis=0, keepdims=True))
        total = None
        for d, s in enumerate(tiles):
            start = pl.multiple_of((n_before + d) * tk, tk)
            part = _dot(value_operand(h, start), jnp.exp2(s - m).astype(BF16))
            total = part if total is None else total + part
        acc_ref[h] = total
        m0.append(m)
    set_reference(m0)
    if early_first:
        first_shift = [jnp.where(n_before > 0, m, -NEG_BIG) for m in m0]

        def absorb_first_head(s_ref, j, h):
            start = pl.multiple_of(j * tk, tk)
            p = jnp.exp2(s_ref[h] - first_shift[h])
            acc_ref[h] = acc_ref[h] + _dot(value_operand(h, start), p.astype(BF16))

        advance(sa_ref, 1, sb_ref, 0, absorb_first_head)
        pipelined(1, n_before, absorb_referenced_head, preloaded=True)
    else:
        pipelined(0, n_before, absorb_referenced_head)

    worst = jnp.max(jnp.abs(acc_ref[...].reshape(n_heads * (HEAD_DIM + BF16_ROWS), tile)))

    @pl.when(jnp.logical_not(worst < jnp.finfo(F32).max))
    def _():
        build_query_operands()
        m_ref[...] = jnp.full(m_ref.shape, NEG_BIG, F32)
        acc_ref[...] = jnp.zeros(acc_ref.shape, F32)
        pipelined(0, n_before, absorb_head)
        for d, s_ref in enumerate(diag_refs):
            scores_into(s_ref, n_before + d)
        for d, s_ref in enumerate(diag_refs):
            absorb(s_ref, n_before + d, absorb_head, mask=causal[d])

    for pair in range(n_heads // 2):
        oT = jnp.concatenate([acc_ref[h, :HEAD_DIM, :] / acc_ref[h, HEAD_DIM:HEAD_DIM + 1, :]
                              for h in (2 * pair, 2 * pair + 1)], axis=0)
        cols = slice(pair * pair_w, (pair + 1) * pair_w)
        o_ref[:, cols] = (oT.T * sg_ref[:, cols].astype(F32)).astype(BF16)


def _attn_call(qte, ke, vT, sg):
    bsz, cw, seq = vT.shape
    tile = min(ATTN_TILE, seq)
    tk = min(ATTN_KEY_TILE, seq)
    assert seq % tile == 0 and seq % tk == 0 and (tile % tk == 0 or tk % tile == 0)
    assert seq // tk >= 2
    n_heads = cw // HEAD_DIM
    kw = cw + LANES
    return pl.pallas_call(
        functools.partial(_attn_kernel, tq=tile, tk=tk, cw=cw, n_heads=n_heads),
        grid=(bsz, seq // tile),
        in_specs=[
            pl.BlockSpec((None, kw, tile), lambda b, i: (b, 0, i)),
            pl.BlockSpec((None, seq, kw), lambda b, i: (b, 0, 0)),
            pl.BlockSpec((None, cw, seq), lambda b, i: (b, 0, 0)),
            pl.BlockSpec((None, tile, cw), lambda b, i: (b, i, 0)),
        ],
        out_specs=pl.BlockSpec((None, tile, cw), lambda b, i: (b, i, 0)),
        out_shape=jax.ShapeDtypeStruct((bsz, seq, cw), BF16),
        scratch_shapes=[
            pltpu.VMEM((n_heads, 2 * HEAD_DIM + LANES, tile), BF16),
            pltpu.VMEM((n_heads, 1, tile), F32),
            pltpu.VMEM((n_heads, HEAD_DIM + BF16_ROWS, tile), F32),
            pltpu.VMEM((n_heads, tk, tile), F32),
            pltpu.VMEM((n_heads, tk, tile), F32),
        ],
        compiler_params=pltpu.CompilerParams(dimension_semantics=("arbitrary", "arbitrary"),
                                             vmem_limit_bytes=VMEM_LIMIT_BYTES),
        name="forget_attn",
    )(qte, ke, vT, sg)


def _merge_kernel(x_ref, ha_ref, hb_ref, hc_ref, hm_ref, wg_ref, pa_ref, pb_ref, pc_ref, pm_ref,
                  wo_ref, g_ref, b_ref, o_ref, *, alpha):
    x = x_ref[...]
    xb = x.astype(BF16)
    d = x.shape[-1]
    merged = None
    for n, (h_ref, p_ref) in enumerate(((ha_ref, pa_ref), (hb_ref, pb_ref), (hc_ref, pc_ref), (hm_ref, pm_ref))):
        gate = _sigmoid(_dot(xb, wg_ref[:, n * d:(n + 1) * d]))
        term = gate * _dot(h_ref[...], p_ref[...])
        merged = term if merged is None else merged + term
    out = _dot(merged.astype(BF16), wo_ref[...])
    o_ref[...] = _layer_norm(alpha * x + out, g_ref[...], b_ref[...])


def _merge_call(x, ha, hb, hc, hm, layer, w, alpha):
    bsz, seq, d = x.shape
    tm = min(TOKEN_TILE, seq)
    row = lambda width: pl.BlockSpec((None, tm, width), lambda i, s: (i, s, 0))
    in_specs = [row(d), row(ha.shape[-1]), row(hb.shape[-1]), row(hc.shape[-1]), row(hm.shape[-1])]
    names = ("wg", "pa", "pb", "pc", "pm", "wo", "lng", "lnb")
    in_specs += [_const_spec(w[n].shape[1:], layer) for n in names]
    return pl.pallas_call(
        functools.partial(_merge_kernel, alpha=alpha),
        grid=(bsz, seq // tm), in_specs=in_specs, out_specs=row(d),
        out_shape=jax.ShapeDtypeStruct((bsz, seq, d), F32),
        compiler_params=pltpu.CompilerParams(dimension_semantics=("arbitrary", "arbitrary"),
                                             vmem_limit_bytes=VMEM_LIMIT_BYTES),
        name="merge_out_norm",
    )(x, ha, hb, hc, hm, *[w[n] for n in names])


def _prepare_weights(w_in, b_forget, conv_a_w, conv_a_b, ln_a_g, ln_a_b, conv_b_w, p_a, p_b, p_c, p_m,
                     w_out, ln_g, ln_b, dims):
    aw, bw, cw, mw, nch, d = dims["aw"], dims["bw"], dims["cw"], dims["mw"], dims["n_cheads"], dims["d"]
    depth = w_in.shape[0]
    cuts = [0]
    for width in (3 * aw, 4 * bw, cw, cw, cw, nch, cw, 2 * mw, 4 * d):
        cuts.append(cuts[-1] + width)
    assert cuts[-1] == w_in.shape[-1]
    seg = lambda n: w_in[:, :, cuts[n]:cuts[n + 1]]
    wq, wk, wv, wf, wcg = seg(2), seg(3), seg(4), seg(5), seg(6)
    wf_cols = jnp.pad(wf, ((0, 0), (0, 0), (0, LANES - nch)))
    wf_rows = jnp.pad(jnp.swapaxes(wf, 1, 2), ((0, 0), (0, BF16_ROWS - nch), (0, 0)))
    bf_pad_n = jnp.pad(b_forget, ((0, 0), (0, LANES - nch)))[:, None, :]
    bf_pad_t = jnp.broadcast_to(jnp.pad(b_forget, ((0, 0), (0, BF16_ROWS - nch)))[:, :, None],
                                (depth, BF16_ROWS, LANES))
    return {
        "wa": seg(0).astype(BF16), "wb": seg(1).astype(BF16),
        "wkc": jnp.concatenate([wk, wcg, wf_cols], axis=-1).astype(BF16),
        "wqvT": jnp.concatenate([jnp.swapaxes(wq, 1, 2), jnp.swapaxes(wv, 1, 2), wf_rows], axis=1).astype(BF16),
        "wm": seg(7).astype(BF16), "wg": seg(8).astype(BF16),
        "bfn": bf_pad_n, "bfT": bf_pad_t,
        "caw": jnp.broadcast_to(conv_a_w[:, :, None, :], conv_a_w.shape[:2] + (SUBLANES, aw)),
        "cab": conv_a_b[:, None, :], "lag": ln_a_g[:, None, :], "lab": ln_a_b[:, None, :],
        "cbw": conv_b_w,
        "pa": p_a.astype(BF16), "pb": p_b.astype(BF16), "pc": p_c.astype(BF16), "pm": p_m.astype(BF16),
        "wo": w_out.astype(BF16), "lng": ln_g[:, None, :], "lnb": ln_b[:, None, :],
    }


def _decay_selectors(n_heads):
    assert BF16_ROWS * n_heads <= LANES and n_heads <= BF16_ROWS
    selk = np.zeros((3 * LANES, LANES), np.float32)
    selq = np.zeros((LANES, 3 * BF16_ROWS), np.float32)
    for h in range(n_heads):
        for r in range(3):
            selk[r * LANES + h, BF16_ROWS * h + r] = -1.0
            selq[BF16_ROWS * h + 3 + r, r * BF16_ROWS + h] = 1.0
    return jnp.asarray(selk, BF16), jnp.asarray(selq, BF16)


def kernel(x, mem, w_in, b_forget, conv_a_w, conv_a_b, ln_a_g, ln_a_b, conv_b_w, w_kv_mem, mem_ln_g, mem_ln_b,
           p_a, p_b, p_c, p_m, w_out, ln_g, ln_b):
    bsz, seq, d = x.shape
    depth = w_in.shape[0]
    dims = {"d": d, "aw": conv_a_w.shape[-1], "bw": conv_b_w.shape[-1], "cw": p_c.shape[1], "mw": p_m.shape[1],
            "n_cheads": b_forget.shape[-1]}
    assert dims["cw"] == dims["n_cheads"] * HEAD_DIM and dims["n_cheads"] % 2 == 0
    assert dims["mw"] % HEAD_DIM == 0 and seq % min(TOKEN_TILE, seq) == 0 and seq % min(ATTN_TILE, seq) == 0
    assert conv_a_w.shape[1] - 1 <= A_HIST and conv_b_w.shape[1] - 1 <= B_HIST
    alpha = (2.0 * depth) ** 0.25
    w = _prepare_weights(w_in, b_forget, conv_a_w, conv_a_b, ln_a_g, ln_a_b, conv_b_w, p_a, p_b, p_c, p_m,
                         w_out, ln_g, ln_b, dims)
    mw = dims["mw"]
    wkT_mem = jnp.swapaxes(w_kv_mem[:, :, :mw], 1, 2).astype(BF16)
    wv_mem = w_kv_mem[:, :, mw:].astype(BF16)
    mkT, mv = _mem_call(mem, mem_ln_g[None, :], mem_ln_b[None, :], wkT_mem, wv_mem)
    tm = min(TOKEN_TILE, seq)
    tril = jnp.tril(jnp.ones((tm, tm), BF16))
    triu = jnp.triu(jnp.ones((tm, tm), BF16))
    selk, selq = _decay_selectors(dims["n_cheads"])
    for layer in range(depth):
        ha, hb, ke, sgc, qte, vT, hm = _proj_call(x, layer, w, mkT, mv, tril, triu, selk, selq, dims)
        hc = _attn_call(qte, ke, vT, sgc)
        x = _merge_call(x, ha, hb, hc, hm, layer, w, alpha)
    return x
```

```python
import functools

import jax
import jax.numpy as jnp
import numpy as np
from jax import lax
from jax.experimental import pallas as pl
from jax.experimental.pallas import tpu as pltpu

F32 = jnp.float32
BF16 = jnp.bfloat16

LN_EPS = 1e-5
HEAD_DIM = 64
LANES = 128
SUBLANES = 8
BF16_ROWS = 16
VMEM_LIMIT_BYTES = 56 * 1024 * 1024
NEG_BIG = -1e30
LOG2E = 1.4426950408889634

TOKEN_TILE = 512
ATTN_TILE = 256
ATTN_KEY_TILE = 512
CONV_TAPS_PER_PASS = 4
A_HIST = 32
B_HIST = 8


def _dot(a, b):
    return jnp.dot(a, b, preferred_element_type=F32)


def _dot_nt(a, b):
    return lax.dot_general(a, b, (((1,), (1,)), ((), ())), preferred_element_type=F32)


def _sigmoid(v):
    return 1.0 / (1.0 + jnp.exp(-v))


def _silu(v):
    return v * _sigmoid(v)


def _log_sigmoid(v):
    return jnp.minimum(v, 0.0) - jnp.log1p(jnp.exp(-jnp.abs(v)))


def _layer_norm(v, g, b):
    mu = jnp.mean(v, axis=-1, keepdims=True)
    c = v - mu
    var = jnp.mean(c * c, axis=-1, keepdims=True)
    return c * lax.rsqrt(var + LN_EPS) * g + b


def _split3(v):
    hi = v.astype(BF16)
    r1 = v - hi.astype(F32)
    mid = r1.astype(BF16)
    lo = (r1 - mid.astype(F32)).astype(BF16)
    return hi, mid, lo


def _const_spec(shape, layer):
    nd = len(shape)
    return pl.BlockSpec((None,) + tuple(shape), lambda *_, l=layer, nd=nd: (l,) + (0,) * nd,
                        pipeline_mode=pl.Buffered(1))


def _mem_kernel(mem_ref, g_ref, b_ref, wkT_ref, wv_ref, mkT_ref, mv_ref, *, n_heads):
    mem_n = _layer_norm(mem_ref[...], g_ref[...], b_ref[...]).astype(BF16)
    mkT = _dot_nt(wkT_ref[...], mem_n)
    mv = _dot(mem_n, wv_ref[...])
    row_head = lax.broadcasted_iota(jnp.int32, mkT.shape, 0) // HEAD_DIM
    col_head = lax.broadcasted_iota(jnp.int32, mv.shape, 1) // HEAD_DIM
    for h in range(n_heads):
        mkT_ref[h] = jnp.where(row_head == h, mkT, 0.0).astype(BF16)
        mv_ref[h] = jnp.where(col_head == h, mv, 0.0).astype(BF16)


def _mem_call(mem, g, b, wkT, wv):
    bsz, mlen, d = mem.shape
    depth, mw, _ = wkT.shape
    n_heads = mw // HEAD_DIM
    out = jax.ShapeDtypeStruct((bsz, depth, n_heads, mw, mlen), BF16), \
        jax.ShapeDtypeStruct((bsz, depth, n_heads, mlen, mw), BF16)
    return pl.pallas_call(
        functools.partial(_mem_kernel, n_heads=n_heads),
        grid=(bsz, depth),
        in_specs=[
            pl.BlockSpec((None, mlen, d), lambda i, l: (i, 0, 0)),
            pl.BlockSpec((1, d), lambda i, l: (0, 0)),
            pl.BlockSpec((1, d), lambda i, l: (0, 0)),
            pl.BlockSpec((None, mw, d), lambda i, l: (l, 0, 0)),
            pl.BlockSpec((None, d, mw), lambda i, l: (l, 0, 0)),
        ],
        out_specs=[
            pl.BlockSpec((None, None, n_heads, mw, mlen), lambda i, l: (i, l, 0, 0, 0)),
            pl.BlockSpec((None, None, n_heads, mlen, mw), lambda i, l: (i, l, 0, 0, 0)),
        ],
        out_shape=out,
        compiler_params=pltpu.CompilerParams(dimension_semantics=("arbitrary", "arbitrary"),
                                             vmem_limit_bytes=VMEM_LIMIT_BYTES),
        name="mem_kv",
    )(mem, g, b, wkT, wv)


def _proj_kernel(x_ref, wa_ref, wb_ref, wkc_ref, wqvT_ref, wm_ref, bfn_ref, bfT_ref,
                 caw_ref, cab_ref, lag_ref, lab_ref, cbw_ref, mkT_ref, mv_ref, tril_ref, triu_ref,
                 selk_ref, selq_ref,
                 ha_ref, hb_ref, ke_ref, sgc_ref, qte_ref, vT_ref, hm_ref,
                 abuf, ashift, aconv, bbuf, carry_n, carry_t,
                 *, tm, aw, bw, cw, mw, n_cheads, taps_a, taps_b):
    @pl.when(pl.program_id(1) == 0)
    def _():
        abuf[0:A_HIST, :] = jnp.zeros((A_HIST, aw), F32)
        bbuf[0:B_HIST, :] = jnp.zeros((B_HIST, bw), F32)
        carry_n[...] = jnp.zeros_like(carry_n)
        carry_t[...] = jnp.zeros_like(carry_t)

    xb = x_ref[...].astype(BF16)

    xm = _dot(xb, wm_ref[...])
    xa = _dot(xb, wa_ref[...])
    mq = (xm[:, :mw] * (HEAD_DIM ** -0.5)).astype(BF16)
    mem_scores = [_dot(mq, mkT_ref[h]) for h in range(mw // HEAD_DIM)]
    xbm = _dot(xb, wb_ref[...])
    xk = _dot(xb, wkc_ref[...])
    qv = _dot_nt(wqvT_ref[...], xb)

    abuf[A_HIST:A_HIST + tm, :] = xa[:, :aw] * _sigmoid(xa[:, aw:2 * aw])
    n_shift_rows = tm + A_HIST - SUBLANES
    for sh in range(1, SUBLANES):
        ashift[sh - 1] = abuf[sh:sh + n_shift_rows, :]
    first = A_HIST - (taps_a - 1)
    for t0 in range(0, taps_a, CONV_TAPS_PER_PASS):
        acc = jnp.broadcast_to(cab_ref[...], (tm, aw)) if t0 == 0 else aconv[...]
        for t in range(t0, min(t0 + CONV_TAPS_PER_PASS, taps_a)):
            blk, sh = divmod(first + t, SUBLANES)
            start = blk * SUBLANES
            rows = abuf[start:start + tm, :] if sh == 0 else ashift[sh - 1, start:start + tm, :]
            acc = acc + jnp.tile(caw_ref[t], (tm // SUBLANES, 1)) * rows
        aconv[...] = acc
    abuf[0:A_HIST, :] = abuf[tm:tm + A_HIST, :]
    a = _silu(_layer_norm(aconv[...], lag_ref[...], lab_ref[...]))
    ha_ref[...] = (a * _silu(xa[:, 2 * aw:])).astype(BF16)

    om = jnp.zeros((tm, mw), F32)
    for h, s in enumerate(mem_scores):
        e = jnp.exp(s - jnp.max(s, axis=-1, keepdims=True))
        p = e / jnp.sum(e, axis=-1, keepdims=True)
        om = om + _dot(p.astype(BF16), mv_ref[h])
    hm_ref[...] = (om * _silu(xm[:, mw:])).astype(BF16)

    bbuf[B_HIST:B_HIST + tm, :] = xbm[:, 2 * bw:3 * bw] * xbm[:, :bw]
    firstb = B_HIST - (taps_b - 1)
    convb = cbw_ref[0:1, :] * bbuf[firstb:firstb + tm, :]
    for t in range(1, taps_b):
        convb = convb + cbw_ref[t:t + 1, :] * bbuf[firstb + t:firstb + t + tm, :]
    bbuf[0:B_HIST, :] = bbuf[tm:tm + B_HIST, :]
    hb_ref[...] = (xbm[:, bw:2 * bw] * convb * _silu(xbm[:, 3 * bw:])).astype(BF16)

    sgc_ref[...] = _silu(xk[:, cw:2 * cw]).astype(BF16)
    lf = _log_sigmoid(xk[:, 2 * cw:] + bfn_ref[...])
    cums = _dot(tril_ref[...], jnp.concatenate(_split3(lf), axis=1))
    cum = cums[:, :LANES] + cums[:, LANES:2 * LANES] + cums[:, 2 * LANES:] + carry_n[...]
    carry_n[...] = cum[tm - 1:tm, :]
    parts = jnp.concatenate(_split3(cum * LOG2E), axis=1)
    lane = lax.broadcasted_iota(jnp.int32, (tm, LANES), 1)
    in_slab = lane & (BF16_ROWS - 1)
    ones_cols = jnp.where((in_slab >= 3) & (in_slab < 6) & (lane < BF16_ROWS * n_cheads), 1.0, 0.0)
    ke_ref[:, :cw] = xk[:, :cw].astype(BF16)
    ke_ref[:, cw:] = (_dot(parts, selk_ref[...]) + ones_cols).astype(BF16)

    vT_ref[...] = qv[cw:2 * cw].astype(BF16)
    lft = _log_sigmoid(qv[2 * cw:] + bfT_ref[...][:, 0:1])
    cumts = _dot(jnp.concatenate(_split3(lft), axis=0), triu_ref[...])
    cumt = cumts[:BF16_ROWS] + cumts[BF16_ROWS:2 * BF16_ROWS] + cumts[2 * BF16_ROWS:] + carry_t[...][:, 0:1]
    carry_t[...] = jnp.broadcast_to(cumt[:, tm - 1:tm], carry_t.shape)
    parts_t = jnp.concatenate(_split3(cumt * LOG2E), axis=0)
    sub = lax.broadcasted_iota(jnp.int32, (LANES, tm), 0)
    ones_rows = jnp.where(((sub & (BF16_ROWS - 1)) < 3) & (sub < BF16_ROWS * n_cheads), 1.0, 0.0)
    qte_ref[:cw, :] = (qv[:cw] * (LOG2E * HEAD_DIM ** -0.5)).astype(BF16)
    qte_ref[cw:, :] = (_dot(selq_ref[...], parts_t) + ones_rows).astype(BF16)


def _proj_call(x, layer, w, mkT, mv, tril, triu, selk, selq, dims):
    bsz, seq, d = x.shape
    tm = min(TOKEN_TILE, seq)
    aw, bw, cw, mw, nch = dims["aw"], dims["bw"], dims["cw"], dims["mw"], dims["n_cheads"]
    taps_a, taps_b = w["caw"].shape[1], w["cbw"].shape[1]
    n_heads_m = mw // HEAD_DIM
    mlen = mv.shape[3]
    row = lambda width: pl.BlockSpec((None, tm, width), lambda i, s: (i, s, 0))
    col = lambda height: pl.BlockSpec((None, height, tm), lambda i, s: (i, 0, s))
    in_specs = [
        row(d),
        _const_spec(w["wa"].shape[1:], layer), _const_spec(w["wb"].shape[1:], layer),
        _const_spec(w["wkc"].shape[1:], layer), _const_spec(w["wqvT"].shape[1:], layer),
        _const_spec(w["wm"].shape[1:], layer),
        _const_spec(w["bfn"].shape[1:], layer), _const_spec(w["bfT"].shape[1:], layer),
        _const_spec(w["caw"].shape[1:], layer), _const_spec(w["cab"].shape[1:], layer),
        _const_spec(w["lag"].shape[1:], layer), _const_spec(w["lab"].shape[1:], layer),
        _const_spec(w["cbw"].shape[1:], layer),
        pl.BlockSpec((None, None, n_heads_m, mw, mlen), lambda i, s, l=layer: (i, l, 0, 0, 0)),
        pl.BlockSpec((None, None, n_heads_m, mlen, mw), lambda i, s, l=layer: (i, l, 0, 0, 0)),
        pl.BlockSpec((tm, tm), lambda i, s: (0, 0), pipeline_mode=pl.Buffered(1)),
        pl.BlockSpec((tm, tm), lambda i, s: (0, 0), pipeline_mode=pl.Buffered(1)),
        pl.BlockSpec(selk.shape, lambda i, s: (0, 0), pipeline_mode=pl.Buffered(1)),
        pl.BlockSpec(selq.shape, lambda i, s: (0, 0), pipeline_mode=pl.Buffered(1)),
    ]
    out_specs = [row(aw), row(bw), row(cw + LANES), row(cw), col(cw + LANES), col(cw), row(mw)]
    out_shape = [
        jax.ShapeDtypeStruct((bsz, seq, aw), BF16), jax.ShapeDtypeStruct((bsz, seq, bw), BF16),
        jax.ShapeDtypeStruct((bsz, seq, cw + LANES), BF16), jax.ShapeDtypeStruct((bsz, seq, cw), BF16),
        jax.ShapeDtypeStruct((bsz, cw + LANES, seq), BF16), jax.ShapeDtypeStruct((bsz, cw, seq), BF16),
        jax.ShapeDtypeStruct((bsz, seq, mw), BF16),
    ]
    scratch = [
        pltpu.VMEM((A_HIST + tm, aw), F32),
        pltpu.VMEM((SUBLANES - 1, tm + A_HIST - SUBLANES, aw), F32),
        pltpu.VMEM((tm, aw), F32),
        pltpu.VMEM((B_HIST + tm, bw), F32),
        pltpu.VMEM((1, LANES), F32),
        pltpu.VMEM((BF16_ROWS, LANES), F32),
    ]
    kern = functools.partial(_proj_kernel, tm=tm, aw=aw, bw=bw, cw=cw, mw=mw, n_cheads=nch,
                             taps_a=taps_a, taps_b=taps_b)
    return pl.pallas_call(
        kern, grid=(bsz, seq // tm), in_specs=in_specs, out_specs=out_specs, out_shape=out_shape,
        scratch_shapes=scratch,
        compiler_params=pltpu.CompilerParams(dimension_semantics=("arbitrary", "arbitrary"),
                                             vmem_limit_bytes=VMEM_LIMIT_BYTES),
        name="proj_branches",
    )(x, w["wa"], w["wb"], w["wkc"], w["wqvT"], w["wm"], w["bfn"], w["bfT"],
      w["caw"], w["cab"], w["lag"], w["lab"], w["cbw"], mkT, mv, tril, triu, selk, selq)


def _attn_kernel(qte_ref, ke_ref, vT_ref, sg_ref, o_ref, qh_ref, m_ref, acc_ref, sa_ref, sb_ref,
                 *, tq, tk, cw, n_heads):
    i = pl.program_id(1)
    n_diag = max(tq // tk, 1)
    n_before = (i * tq) // tk
    tile = tq
    pair_w = 2 * HEAD_DIM
    ones_rows = jnp.ones((BF16_ROWS, tk), BF16)
    key_pos = lax.broadcasted_iota(jnp.int32, (tk, tq), 0) + n_before * tk
    query_pos = lax.broadcasted_iota(jnp.int32, (tk, tq), 1) + i * tq
    causal = [key_pos + d * tk <= query_pos for d in range(n_diag)]

    def build_query_operands():
        feat_row = lax.broadcasted_iota(jnp.int32, (pair_w, tile), 0)
        for h in range(n_heads):
            pair, hh = divmod(h, 2)
            qpair = qte_ref[pair * pair_w:(pair + 1) * pair_w, :]
            own = (feat_row >= hh * HEAD_DIM) & (feat_row < (hh + 1) * HEAD_DIM)
            qh_ref[h, :pair_w, :] = jnp.where(own, qpair, jnp.zeros_like(qpair))
            qh_ref[h, pair_w:, :] = jnp.zeros((LANES, tile), BF16)
            slab = slice(BF16_ROWS * h, BF16_ROWS * (h + 1))
            qh_ref[h, pair_w + slab.start:pair_w + slab.stop, :] = qte_ref[cw + slab.start:cw + slab.stop, :]

    def set_reference(shift):
        slab_row = lax.broadcasted_iota(jnp.int32, (BF16_ROWS, tile), 0)
        for h in range(n_heads):
            lo_row = cw + BF16_ROWS * h
            slab = qte_ref[lo_row:lo_row + BF16_ROWS, :].astype(F32)
            parts = _split3(slab[3:4, :] + slab[4:5, :] + slab[5:6, :] - shift[h])
            new = jnp.where(slab_row < 3, 1.0, 0.0)
            for r in range(3):
                new = jnp.where(slab_row == 3 + r, parts[r].astype(F32), new)
            qh_ref[h, pair_w + BF16_ROWS * h:pair_w + BF16_ROWS * (h + 1), :] = new.astype(BF16)

    def value_operand(h, start):
        return jnp.concatenate([vT_ref[h * HEAD_DIM:(h + 1) * HEAD_DIM, pl.ds(start, tk)], ones_rows], axis=0)

    def score_head(s_ref, j, h):
        start = pl.multiple_of(j * tk, tk)
        pair = h // 2
        kblk = jnp.concatenate([ke_ref[pl.ds(start, tk), pair * pair_w:(pair + 1) * pair_w],
                                ke_ref[pl.ds(start, tk), cw:]], axis=1)
        s_ref[h] = _dot(kblk, qh_ref[h])

    def scores_into(s_ref, j):
        for h in range(n_heads):
            score_head(s_ref, j, h)

    def absorb_head(s_ref, j, h, mask=None):
        start = pl.multiple_of(j * tk, tk)
        s = s_ref[h]
        if mask is not None:
            s = jnp.where(mask, s, NEG_BIG)
        m_old = m_ref[h]
        m_new = jnp.maximum(m_old, jnp.max(s, axis=0, keepdims=True))
        p = jnp.exp2(s - m_new)
        alpha = jnp.exp2(m_old - m_new)
        m_ref[h] = m_new
        acc_ref[h] = alpha * acc_ref[h] + _dot(value_operand(h, start), p.astype(BF16))

    def absorb_referenced_head(s_ref, j, h):
        start = pl.multiple_of(j * tk, tk)
        acc_ref[h] = acc_ref[h] + _dot(value_operand(h, start), jnp.exp2(s_ref[h]).astype(BF16))

    def absorb(s_ref, j, absorb_fn, **kw):
        for h in range(n_heads):
            absorb_fn(s_ref, j, h, **kw)

    def advance(nxt_ref, j_next, cur_ref, j_cur, absorb_fn):
        for h in range(n_heads):
            score_head(nxt_ref, j_next, h)
            absorb_fn(cur_ref, j_cur, h)

    def pipelined(n_blocks, absorb_fn):
        def two_blocks(jj, carry):
            j = 2 * jj
            advance(sb_ref, j + 1, sa_ref, j, absorb_fn)
            advance(sa_ref, j + 2, sb_ref, j + 1, absorb_fn)
            return carry

        @pl.when(n_blocks > 0)
        def _():
            scores_into(sa_ref, 0)

        lax.fori_loop(0, jnp.maximum(n_blocks - 1, 0) // 2, two_blocks, 0)

        @pl.when(n_blocks % 2 == 1)
        def _():
            absorb(sa_ref, n_blocks - 1, absorb_fn)

        @pl.when((n_blocks % 2 == 0) & (n_blocks > 0))
        def _():
            advance(sb_ref, n_blocks - 1, sa_ref, n_blocks - 2, absorb_fn)
            absorb(sb_ref, n_blocks - 1, absorb_fn)

    assert n_diag <= 2
    diag_refs = (sa_ref, sb_ref)[:n_diag]
    build_query_operands()
    for d, s_ref in enumerate(diag_refs):
        scores_into(s_ref, n_before + d)
    m0 = []
    for h in range(n_heads):
        tiles = [jnp.where(causal[d], s_ref[h], NEG_BIG) for d, s_ref in enumerate(diag_refs)]
        m = jnp.max(tiles[0], axis=0, keepdims=True)
        for s in tiles[1:]:
            m = jnp.maximum(m, jnp.max(s, axis=0, keepdims=True))
        total = None
        for d, s in enumerate(tiles):
            start = pl.multiple_of((n_before + d) * tk, tk)
            part = _dot(value_operand(h, start), jnp.exp2(s - m).astype(BF16))
            total = part if total is None else total + part
        acc_ref[h] = total
        m0.append(m)
    set_reference(m0)
    pipelined(n_before, absorb_referenced_head)

    worst = jnp.max(jnp.abs(acc_ref[...].reshape(n_heads * (HEAD_DIM + BF16_ROWS), tile)))

    @pl.when(jnp.logical_not(worst < jnp.finfo(F32).max))
    def _():
        build_query_operands()
        m_ref[...] = jnp.full(m_ref.shape, NEG_BIG, F32)
        acc_ref[...] = jnp.zeros(acc_ref.shape, F32)
        pipelined(n_before, absorb_head)
        for d, s_ref in enumerate(diag_refs):
            scores_into(s_ref, n_before + d)
        for d, s_ref in enumerate(diag_refs):
            absorb(s_ref, n_before + d, absorb_head, mask=causal[d])

    for pair in range(n_heads // 2):
        oT = jnp.concatenate([acc_ref[h, :HEAD_DIM, :] / acc_ref[h, HEAD_DIM:HEAD_DIM + 1, :]
                              for h in (2 * pair, 2 * pair + 1)], axis=0)
        cols = slice(pair * pair_w, (pair + 1) * pair_w)
        o_ref[:, cols] = (oT.T * sg_ref[:, cols].astype(F32)).astype(BF16)


def _attn_call(qte, ke, vT, sg):
    bsz, cw, seq = vT.shape
    tile = min(ATTN_TILE, seq)
    tk = min(ATTN_KEY_TILE, seq)
    assert seq % tile == 0 and seq % tk == 0 and (tile % tk == 0 or tk % tile == 0)
    n_heads = cw // HEAD_DIM
    kw = cw + LANES
    return pl.pallas_call(
        functools.partial(_attn_kernel, tq=tile, tk=tk, cw=cw, n_heads=n_heads),
        grid=(bsz, seq // tile),
        in_specs=[
            pl.BlockSpec((None, kw, tile), lambda b, i: (b, 0, i)),
            pl.BlockSpec((None, seq, kw), lambda b, i: (b, 0, 0)),
            pl.BlockSpec((None, cw, seq), lambda b, i: (b, 0, 0)),
            pl.BlockSpec((None, tile, cw), lambda b, i: (b, i, 0)),
        ],
        out_specs=pl.BlockSpec((None, tile, cw), lambda b, i: (b, i, 0)),
        out_shape=jax.ShapeDtypeStruct((bsz, seq, cw), BF16),
        scratch_shapes=[
            pltpu.VMEM((n_heads, 2 * HEAD_DIM + LANES, tile), BF16),
            pltpu.VMEM((n_heads, 1, tile), F32),
            pltpu.VMEM((n_heads, HEAD_DIM + BF16_ROWS, tile), F32),
            pltpu.VMEM((n_heads, tk, tile), F32),
            pltpu.VMEM((n_heads, tk, tile), F32),
        ],
        compiler_params=pltpu.CompilerParams(dimension_semantics=("arbitrary", "arbitrary"),
                                             vmem_limit_bytes=VMEM_LIMIT_BYTES),
        name="forget_attn",
    )(qte, ke, vT, sg)


def _merge_kernel(x_ref, ha_ref, hb_ref, hc_ref, hm_ref, wg_ref, pa_ref, pb_ref, pc_ref, pm_ref,
                  wo_ref, g_ref, b_ref, o_ref, *, alpha):
    x = x_ref[...]
    xb = x.astype(BF16)
    d = x.shape[-1]
    merged = None
    for n, (h_ref, p_ref) in enumerate(((ha_ref, pa_ref), (hb_ref, pb_ref), (hc_ref, pc_ref), (hm_ref, pm_ref))):
        gate = _sigmoid(_dot(xb, wg_ref[:, n * d:(n + 1) * d]))
        term = gate * _dot(h_ref[...], p_ref[...])
        merged = term if merged is None else merged + term
    out = _dot(merged.astype(BF16), wo_ref[...])
    o_ref[...] = _layer_norm(alpha * x + out, g_ref[...], b_ref[...])


def _merge_call(x, ha, hb, hc, hm, layer, w, alpha):
    bsz, seq, d = x.shape
    tm = min(TOKEN_TILE, seq)
    row = lambda width: pl.BlockSpec((None, tm, width), lambda i, s: (i, s, 0))
    in_specs = [row(d), row(ha.shape[-1]), row(hb.shape[-1]), row(hc.shape[-1]), row(hm.shape[-1])]
    names = ("wg", "pa", "pb", "pc", "pm", "wo", "lng", "lnb")
    in_specs += [_const_spec(w[n].shape[1:], layer) for n in names]
    return pl.pallas_call(
        functools.partial(_merge_kernel, alpha=alpha),
        grid=(bsz, seq // tm), in_specs=in_specs, out_specs=row(d),
        out_shape=jax.ShapeDtypeStruct((bsz, seq, d), F32),
        compiler_params=pltpu.CompilerParams(dimension_semantics=("arbitrary", "arbitrary"),
                                             vmem_limit_bytes=VMEM_LIMIT_BYTES),
        name="merge_out_norm",
    )(x, ha, hb, hc, hm, *[w[n] for n in names])


def _prepare_weights(w_in, b_forget, conv_a_w, conv_a_b, ln_a_g, ln_a_b, conv_b_w, p_a, p_b, p_c, p_m,
                     w_out, ln_g, ln_b, dims):
    aw, bw, cw, mw, nch, d = dims["aw"], dims["bw"], dims["cw"], dims["mw"], dims["n_cheads"], dims["d"]
    depth = w_in.shape[0]
    cuts = [0]
    for width in (3 * aw, 4 * bw, cw, cw, cw, nch, cw, 2 * mw, 4 * d):
        cuts.append(cuts[-1] + width)
    assert cuts[-1] == w_in.shape[-1]
    seg = lambda n: w_in[:, :, cuts[n]:cuts[n + 1]]
    wq, wk, wv, wf, wcg = seg(2), seg(3), seg(4), seg(5), seg(6)
    wf_cols = jnp.pad(wf, ((0, 0), (0, 0), (0, LANES - nch)))
    wf_rows = jnp.pad(jnp.swapaxes(wf, 1, 2), ((0, 0), (0, BF16_ROWS - nch), (0, 0)))
    bf_pad_n = jnp.pad(b_forget, ((0, 0), (0, LANES - nch)))[:, None, :]
    bf_pad_t = jnp.broadcast_to(jnp.pad(b_forget, ((0, 0), (0, BF16_ROWS - nch)))[:, :, None],
                                (depth, BF16_ROWS, LANES))
    return {
        "wa": seg(0).astype(BF16), "wb": seg(1).astype(BF16),
        "wkc": jnp.concatenate([wk, wcg, wf_cols], axis=-1).astype(BF16),
        "wqvT": jnp.concatenate([jnp.swapaxes(wq, 1, 2), jnp.swapaxes(wv, 1, 2), wf_rows], axis=1).astype(BF16),
        "wm": seg(7).astype(BF16), "wg": seg(8).astype(BF16),
        "bfn": bf_pad_n, "bfT": bf_pad_t,
        "caw": jnp.broadcast_to(conv_a_w[:, :, None, :], conv_a_w.shape[:2] + (SUBLANES, aw)),
        "cab": conv_a_b[:, None, :], "lag": ln_a_g[:, None, :], "lab": ln_a_b[:, None, :],
        "cbw": conv_b_w,
        "pa": p_a.astype(BF16), "pb": p_b.astype(BF16), "pc": p_c.astype(BF16), "pm": p_m.astype(BF16),
        "wo": w_out.astype(BF16), "lng": ln_g[:, None, :], "lnb": ln_b[:, None, :],
    }


def _decay_selectors(n_heads):
    assert BF16_ROWS * n_heads <= LANES and n_heads <= BF16_ROWS
    selk = np.zeros((3 * LANES, LANES), np.float32)
    selq = np.zeros((LANES, 3 * BF16_ROWS), np.float32)
    for h in range(n_heads):
        for r in range(3):
            selk[r * LANES + h, BF16_ROWS * h + r] = -1.0
            selq[BF16_ROWS * h + 3 + r, r * BF16_ROWS + h] = 1.0
    return jnp.asarray(selk, BF16), jnp.asarray(selq, BF16)


def kernel(x, mem, w_in, b_forget, conv_a_w, conv_a_b, ln_a_g, ln_a_b, conv_b_w, w_kv_mem, mem_ln_g, mem_ln_b,
           p_a, p_b, p_c, p_m, w_out, ln_g, ln_b):
    bsz, seq, d = x.shape
    depth = w_in.shape[0]
    dims = {"d": d, "aw": conv_a_w.shape[-1], "bw": conv_b_w.shape[-1], "cw": p_c.shape[1], "mw": p_m.shape[1],
            "n_cheads": b_forget.shape[-1]}
    assert dims["cw"] == dims["n_cheads"] * HEAD_DIM and dims["n_cheads"] % 2 == 0
    assert dims["mw"] % HEAD_DIM == 0 and seq % min(TOKEN_TILE, seq) == 0 and seq % min(ATTN_TILE, seq) == 0
    assert conv_a_w.shape[1] - 1 <= A_HIST and conv_b_w.shape[1] - 1 <= B_HIST
    alpha = (2.0 * depth) ** 0.25
    w = _prepare_weights(w_in, b_forget, conv_a_w, conv_a_b, ln_a_g, ln_a_b, conv_b_w, p_a, p_b, p_c, p_m,
                         w_out, ln_g, ln_b, dims)
    mw = dims["mw"]
    wkT_mem = jnp.swapaxes(w_kv_mem[:, :, :mw], 1, 2).astype(BF16)
    wv_mem = w_kv_mem[:, :, mw:].astype(BF16)
    mkT, mv = _mem_call(mem, mem_ln_g[None, :], mem_ln_b[None, :], wkT_mem, wv_mem)
    tm = min(TOKEN_TILE, seq)
    tril = jnp.tril(jnp.ones((tm, tm), BF16))
    triu = jnp.triu(jnp.ones((tm, tm), BF16))
    selk, selq = _decay_selectors(dims["n_cheads"])
    for layer in range(depth):
        ha, hb, ke, sgc, qte, vT, hm = _proj_call(x, layer, w, mkT, mv, tril, triu, selk, selq, dims)
        hc = _attn_call(qte, ke, vT, sgc)
        x = _merge_call(x, ha, hb, hc, hm, layer, w, alpha)
    return x
```
